```python
import jax, jax.numpy as jnp
from jax import lax
import numpy as np


D_MODEL = 1024
BATCH = 8
SEQ = 4096
DEPTH = 1

A_HEADS = 8
A_HEAD_DIM = 64
A_WIDTH = A_HEADS * A_HEAD_DIM
MOBA_BLOCK = 256
MOBA_TOPK = 3
MOBA_QCHUNK = 128
M_HEADS = 4
M_HEAD_DIM = 128
M_WIDTH = M_HEADS * M_HEAD_DIM
M_CONV = 4
M_CHUNK = 64
N_EXPERTS = 32
TOP_K = 4
D_FF = 1024
SWIGLU_LIMIT = 7.0
SWIGLU_ALPHA = 1.702
MOE_BLOCK = 256
ROPE_THETA = 10000.0
EPS = 1e-6
NEG = -1e30
IN_SIZES = (A_WIDTH, A_WIDTH, A_WIDTH, M_WIDTH, M_WIDTH, M_WIDTH, M_WIDTH, M_HEADS, M_HEADS, D_MODEL, D_MODEL)
IN_SPLITS = tuple(sum(IN_SIZES[:i + 1]) for i in range(len(IN_SIZES) - 1))
IN_COLS = sum(IN_SIZES)

kernel_name = 'hybrid_moba_mlstm_moe_block'


def rmsnorm(x, w):
    xf = x.astype(jnp.float32)
    y = xf * lax.rsqrt(jnp.mean(xf * xf, axis=-1, keepdims=True) + EPS)
    return (y * w.astype(jnp.float32)).astype(x.dtype)


def rope(t):
    s, dh = t.shape[-2], t.shape[-1]
    inv = ROPE_THETA ** (-jnp.arange(0, dh, 2, dtype=jnp.float32) / dh)
    ang = jnp.arange(s, dtype=jnp.float32)[:, None] * inv[None, :]
    cos = jnp.concatenate([jnp.cos(ang), jnp.cos(ang)], axis=-1)
    sin = jnp.concatenate([jnp.sin(ang), jnp.sin(ang)], axis=-1)
    tf = t.astype(jnp.float32)
    rot = jnp.concatenate([-tf[..., dh // 2:], tf[..., :dh // 2]], axis=-1)
    return (tf * cos + rot * sin).astype(t.dtype)


def causal_dwconv(u, w, b):
    width, ch = w.shape
    y = lax.conv_general_dilated(u, w[:, None, :].astype(u.dtype), window_strides=(1,),
                                 padding=[(width - 1, 0)],
                                 dimension_numbers=('NWC', 'WIO', 'NWC'),
                                 feature_group_count=ch)
    return y + b.astype(u.dtype)


def moba_attention(q, k, v):
    bsz, nh, s, dh = q.shape
    nb = -(-s // MOBA_BLOCK)
    pad = nb * MOBA_BLOCK - s
    kb = jnp.pad(k, ((0, 0), (0, 0), (0, pad), (0, 0))).reshape(bsz, nh, nb, MOBA_BLOCK, dh)
    vb = jnp.pad(v, ((0, 0), (0, 0), (0, pad), (0, 0))).reshape(bsz, nh, nb, MOBA_BLOCK, dh)
    k_mean = jnp.mean(kb.astype(jnp.float32), axis=3)
    q_blk = jnp.arange(s) // MOBA_BLOCK
    past = jnp.arange(nb)[None, :] < q_blk[:, None]
    gate = jnp.einsum('bhsd,bhnd->bhsn', q.astype(jnp.float32), k_mean)
    gate = jnp.where(past, gate, -jnp.inf)
    ksel = min(MOBA_TOPK, nb)
    _, sel = lax.top_k(gate, ksel)
    valid = jnp.arange(ksel)[None, :] < q_blk[:, None]
    nq = s // MOBA_QCHUNK
    qc = q.reshape(bsz, nh, nq, MOBA_QCHUNK, dh).transpose(0, 2, 1, 3, 4)
    selc = sel.reshape(bsz, nh, nq, MOBA_QCHUNK, ksel).transpose(0, 2, 1, 3, 4)
    validc = valid.reshape(nq, MOBA_QCHUNK, ksel)
    scale = dh ** -0.5
    gather_blocks = jax.vmap(lambda blocks, idx: blocks[idx])

    def per_batch(args):
        q_b, sel_b, kb_b, vb_b = args

        def per_chunk(cargs):
            c, q_c, sel_c, valid_c = cargs
            j = (c * MOBA_QCHUNK) // MOBA_BLOCK
            k_sel = gather_blocks(kb_b, sel_c)
            v_sel = gather_blocks(vb_b, sel_c)
            k_own = lax.dynamic_index_in_dim(kb_b, j, axis=1, keepdims=False)
            v_own = lax.dynamic_index_in_dim(vb_b, j, axis=1, keepdims=False)
            q_pos = c * MOBA_QCHUNK + jnp.arange(MOBA_QCHUNK)
            k_pos = j * MOBA_BLOCK + jnp.arange(MOBA_BLOCK)
            s_own = jnp.einsum('hqd,hkd->hqk', q_c, k_own).astype(jnp.float32) * scale
            s_own = jnp.where(k_pos[None, :] <= q_pos[:, None], s_own, NEG)
            s_sel = jnp.einsum('hqd,hqnkd->hqnk', q_c, k_sel).astype(jnp.float32) * scale
            s_sel = jnp.where(valid_c[None, :, :, None], s_sel, NEG)
            scores = jnp.concatenate([s_own, s_sel.reshape(nh, MOBA_QCHUNK, ksel * MOBA_BLOCK)], axis=-1)
            p = jax.nn.softmax(scores, axis=-1).astype(q_c.dtype)
            p_own = p[..., :MOBA_BLOCK]
            p_sel = p[..., MOBA_BLOCK:].reshape(nh, MOBA_QCHUNK, ksel, MOBA_BLOCK)
            return (jnp.einsum('hqk,hkd->hqd', p_own, v_own)
                    + jnp.einsum('hqnk,hqnkd->hqd', p_sel, v_sel))

        return lax.map(per_chunk, (jnp.arange(nq), q_b, sel_b, validc))

    out = lax.map(per_batch, (qc, selc, kb, vb))
    return out.transpose(0, 2, 1, 3, 4).reshape(bsz, nh, s, dh)


def mlstm_chunkwise(q, k, v, i_pre, f_pre):
    bsz, nh, s, dh = q.shape
    nc = s // M_CHUNK
    k = k * (dh ** -0.5)
    log_f = jax.nn.log_sigmoid(f_pre)

    def chunks(t):
        return jnp.moveaxis(t.reshape((bsz, nh, nc, M_CHUNK) + t.shape[3:]), 2, 0)

    causal = jnp.tril(jnp.ones((M_CHUNK, M_CHUNK), dtype=bool))

    def step(carry, xs):
        c_st, n_st, m_st = carry
        q_c, k_c, v_c, i_c, f_c = xs
        b = jnp.cumsum(f_c, axis=-1)
        d = b[..., :, None] - b[..., None, :] + i_c[..., None, :]
        d = jnp.where(causal, d, -jnp.inf)
        g = b + m_st[..., None]
        m_t = jnp.maximum(g, jnp.max(d, axis=-1))
        w_intra = jnp.exp(d - m_t[..., None])
        w_inter = jnp.exp(g - m_t)
        qk = jnp.einsum('bhtd,bhsd->bhts', q_c, k_c) * w_intra
        num = (w_inter[..., None] * jnp.einsum('bhvd,bhtd->bhtv', c_st, q_c)
               + jnp.einsum('bhts,bhsv->bhtv', qk, v_c))
        den = w_inter * jnp.einsum('bhd,bhtd->bht', n_st, q_c) + jnp.sum(qk, axis=-1)
        h = num / jnp.maximum(jnp.abs(den), jnp.exp(-m_t))[..., None]
        b_last = b[..., -1]
        log_w = b_last[..., None] - b + i_c
        m_new = jnp.maximum(b_last + m_st, jnp.max(log_w, axis=-1))
        w_k = jnp.exp(log_w - m_new[..., None])
        decay = jnp.exp(b_last + m_st - m_new)
        c_new = decay[..., None, None] * c_st + jnp.einsum('bhs,bhsv,bhsd->bhvd', w_k, v_c, k_c)
        n_new = decay[..., None] * n_st + jnp.einsum('bhs,bhsd->bhd', w_k, k_c)
        return (c_new, n_new, m_new), h

    init = (jnp.zeros((bsz, nh, dh, dh), jnp.float32),
            jnp.zeros((bsz, nh, dh), jnp.float32),
            jnp.zeros((bsz, nh), jnp.float32))
    _, hs = lax.scan(step, init, (chunks(q), chunks(k), chunks(v), chunks(i_pre), chunks(log_f)))
    return jnp.moveaxis(hs, 0, 2).reshape(bsz, nh, s, dh)


def token_mixer(xn, w_in, m_conv_w, m_conv_b, m_gate_bias, m_head_norm_w, w_branch, w_out):
    bsz, s, _ = xn.shape
    proj = xn @ w_in
    a_q, a_k, a_v, m_q, m_k, m_v, m_o, m_i, m_f, g_a, g_m = jnp.split(proj, IN_SPLITS, axis=-1)

    def heads(t, nh, dh):
        return t.reshape(bsz, s, nh, dh).transpose(0, 2, 1, 3)

    y_a = moba_attention(rope(heads(a_q, A_HEADS, A_HEAD_DIM)),
                         rope(heads(a_k, A_HEADS, A_HEAD_DIM)),
                         heads(a_v, A_HEADS, A_HEAD_DIM))
    y_a = y_a.transpose(0, 2, 1, 3).reshape(bsz, s, A_WIDTH)

    qk = jax.nn.silu(causal_dwconv(jnp.concatenate([m_q, m_k], axis=-1), m_conv_w, m_conv_b))
    m_q, m_k = qk[..., :M_WIDTH], qk[..., M_WIDTH:]
    gates = (jnp.concatenate([m_i, m_f], axis=-1) + m_gate_bias).astype(jnp.float32).transpose(0, 2, 1)
    h_m = mlstm_chunkwise(heads(m_q, M_HEADS, M_HEAD_DIM).astype(jnp.float32),
                          heads(m_k, M_HEADS, M_HEAD_DIM).astype(jnp.float32),
                          heads(m_v, M_HEADS, M_HEAD_DIM).astype(jnp.float32),
                          gates[:, :M_HEADS], gates[:, M_HEADS:])
    h_m = h_m * lax.rsqrt(jnp.mean(h_m * h_m, axis=-1, keepdims=True) + EPS)
    h_m = h_m.transpose(0, 2, 1, 3).reshape(bsz, s, M_WIDTH)
    y_m = (h_m * m_head_norm_w.astype(jnp.float32)).astype(xn.dtype) * jax.nn.sigmoid(m_o)

    p_a = y_a @ w_branch[:A_WIDTH]
    p_m = y_m @ w_branch[A_WIDTH:]
    merged = jax.nn.sigmoid(g_a) * p_a + jax.nn.sigmoid(g_m) * p_m
    return merged @ w_out


def moe_ffn(xn, w_router, b_router, w_mlp1, b_mlp1, w_mlp2, b_mlp2):
    bsz, s, d = xn.shape
    n_tok = bsz * s
    xt = xn.reshape(n_tok, d)
    logits = (xt @ w_router + b_router).astype(jnp.float32)
    top_val, top_idx = lax.top_k(logits, TOP_K)
    gate = jax.nn.softmax(top_val, axis=-1)
    n_asg = n_tok * TOP_K
    e_flat = top_idx.reshape(n_asg).astype(jnp.int32)
    tok_flat = jnp.arange(n_asg, dtype=jnp.int32) // TOP_K
    counts = jnp.bincount(e_flat, length=N_EXPERTS).astype(jnp.int32)
    padded = (counts + MOE_BLOCK - 1) // MOE_BLOCK * MOE_BLOCK
    pad_end = jnp.cumsum(padded)
    pad_start = pad_end - padded
    start = jnp.cumsum(counts) - counts
    order = jnp.argsort(e_flat)
    e_sorted = e_flat[order]
    dest_sorted = (pad_start[e_sorted] + jnp.arange(n_asg, dtype=jnp.int32) - start[e_sorted]).astype(jnp.int32)
    dest = jnp.zeros((n_asg,), jnp.int32).at[order].set(dest_sorted)
    n_blk = -(-n_asg // MOE_BLOCK) + N_EXPERTS
    n_rows = n_blk * MOE_BLOCK
    row_tok = jnp.full((n_rows,), n_tok, jnp.int32).at[dest].set(tok_flat)
    x_rows = jnp.concatenate([xt, jnp.zeros((1, d), xt.dtype)], axis=0)[row_tok].reshape(n_blk, MOE_BLOCK, d)
    blk_expert = jnp.minimum(jnp.searchsorted(pad_end, jnp.arange(n_blk, dtype=jnp.int32) * MOE_BLOCK,
                                              side='right'), N_EXPERTS - 1)

    def expert_block(args):
        xb, e = args
        hdn = xb @ w_mlp1[e] + b_mlp1[e]
        glu = jnp.minimum(hdn[:, :D_FF], SWIGLU_LIMIT)
        lin = jnp.clip(hdn[:, D_FF:], -SWIGLU_LIMIT, SWIGLU_LIMIT)
        act = glu * jax.nn.sigmoid(SWIGLU_ALPHA * glu) * (lin + 1.0)
        return act @ w_mlp2[e] + b_mlp2[e]

    y_rows = lax.map(expert_block, (x_rows, blk_expert)).reshape(n_rows, d)
    y = y_rows[dest].reshape(n_tok, TOP_K, d)
    return jnp.einsum('tk,tkd->td', gate.astype(y.dtype), y).reshape(bsz, s, d)


def setup_inputs(seed: int = 0) -> dict:
    key = jax.random.key(seed)
    ks = jax.random.split(key, 18)
    f32 = jnp.float32
    L = DEPTH

    def nrm(k, shape, scale):
        return jax.random.normal(k, shape, f32) * scale

    return {
        'x': nrm(ks[0], (BATCH, SEQ, D_MODEL), 1.0),
        'norm_mix_w': 1.0 + nrm(ks[1], (L, D_MODEL), 0.02),
        'w_in': nrm(ks[2], (L, D_MODEL, IN_COLS), D_MODEL ** -0.5),
        'm_conv_w': nrm(ks[3], (L, M_CONV, 2 * M_WIDTH), M_CONV ** -0.5),
        'm_conv_b': nrm(ks[4], (L, 2 * M_WIDTH), 0.01),
        'm_gate_bias': jnp.concatenate([nrm(ks[5], (L, M_HEADS), 0.1),
                                        3.0 + 3.0 * jax.random.uniform(ks[6], (L, M_HEADS), f32)], axis=-1),
        'm_head_norm_w': 1.0 + nrm(ks[7], (L, M_WIDTH), 0.02),
        'w_branch': nrm(ks[8], (L, A_WIDTH + M_WIDTH, D_MODEL), A_WIDTH ** -0.5),
        'w_out': nrm(ks[9], (L, D_MODEL, D_MODEL), D_MODEL ** -0.5),
        'norm_ffn_w': 1.0 + nrm(ks[10], (L, D_MODEL), 0.02),
        'w_router': nrm(ks[11], (L, D_MODEL, N_EXPERTS), D_MODEL ** -0.5),
        'b_router': nrm(ks[12], (L, N_EXPERTS), 0.01),
        'w_mlp1': nrm(ks[13], (L, N_EXPERTS, D_MODEL, 2 * D_FF), D_MODEL ** -0.5),
        'b_mlp1': nrm(ks[14], (L, N_EXPERTS, 2 * D_FF), 0.01),
        'w_mlp2': nrm(ks[15], (L, N_EXPERTS, D_FF, D_MODEL), D_FF ** -0.5),
        'b_mlp2': nrm(ks[16], (L, N_EXPERTS, D_MODEL), 0.01),
        'norm_final_w': 1.0 + nrm(ks[17], (D_MODEL,), 0.02),
    }


def reference(x, norm_mix_w, w_in, m_conv_w, m_conv_b, m_gate_bias, m_head_norm_w, w_branch, w_out,
              norm_ffn_w, w_router, b_router, w_mlp1, b_mlp1, w_mlp2, b_mlp2, norm_final_w):
    h = x
    for l in range(DEPTH):
        h = h + token_mixer(rmsnorm(h, norm_mix_w[l]), w_in[l], m_conv_w[l], m_conv_b[l],
                            m_gate_bias[l], m_head_norm_w[l], w_branch[l], w_out[l])
        h = h + moe_ffn(rmsnorm(h, norm_ffn_w[l]), w_router[l], b_router[l],
                        w_mlp1[l], b_mlp1[l], w_mlp2[l], b_mlp2[l])
    return rmsnorm(h, norm_final_w)
```

```python
import functools

import jax
import jax.numpy as jnp
from jax import lax
from jax.experimental import pallas as pl
from jax.experimental.pallas import tpu as pltpu

F32 = jnp.float32
BF16 = jnp.bfloat16
I32 = jnp.int32
HIGHEST = lax.Precision.HIGHEST

D_MODEL = 1024
A_HEADS = 8
A_HEAD_DIM = 64
A_WIDTH = A_HEADS * A_HEAD_DIM
MOBA_BLOCK = 256
MOBA_TOPK = 3
M_HEADS = 4
M_HEAD_DIM = 128
M_WIDTH = M_HEADS * M_HEAD_DIM
M_CONV = 4
N_EXPERTS = 32
TOP_K = 4
D_FF = 1024
SWIGLU_LIMIT = 7.0
SWIGLU_ALPHA = 1.702
MOE_BLOCK = 256
ROPE_THETA = 10000.0
EPS = 1e-6
NEG = -1e30
NEG_INF = float("-inf")

LANES = 128
SUBLANES = 8
VMEM_LIMIT_CAP = 56 * 1024 * 1024

C_AQ, C_AK, C_AV = 0, 512, 1024
C_MQK, C_MV, C_MO = 1536, 2560, 3072
C_GAM, C_GIF, C_END = 3584, 5632, 5760

ROW_TILE = 256
MLSTM_CHUNK = 256


def _params(vmem_bytes, n_axes=1):
    return pltpu.CompilerParams(
        dimension_semantics=("arbitrary",) * n_axes,
        vmem_limit_bytes=int(min(max(vmem_bytes, 16 * 1024 * 1024), VMEM_LIMIT_CAP)))


def _iota(shape, dim):
    return lax.broadcasted_iota(I32, shape, dim)


def _sigmoid(x):
    return 1.0 / (1.0 + jnp.exp(-x))


def _nt_dot(a, b, precision=None):
    return lax.dot_general(a, b, (((1,), (1,)), ((), ())), precision=precision,
                           preferred_element_type=F32)


def _inproj_kernel(x_ref, nw_ref, w_ref, gb_ref, cos_ref, sin_ref,
                   aq_ref, ak_ref, av_ref, km_ref, mqk_ref, mv_ref, mo_ref, gam_ref, gif_ref):
    tm = x_ref.shape[0]
    x = x_ref[...]
    xn = x * lax.rsqrt(jnp.mean(x * x, axis=-1, keepdims=True) + EPS) * nw_ref[...]
    xb = xn.astype(BF16)

    def mm(lo, hi):
        return jnp.dot(xb, w_ref[:, lo:hi], preferred_element_type=F32)

    cos = jnp.concatenate([cos_ref[...]] * 4, axis=1)
    sin = jnp.concatenate([sin_ref[...]] * 4, axis=1)
    lane = _iota((tm, A_WIDTH), 1)
    first_half = (lane & (A_HEAD_DIM - 1)) < (A_HEAD_DIM // 2)

    def rope(t):
        up = pltpu.roll(t, A_WIDTH - A_HEAD_DIM // 2, 1)
        dn = pltpu.roll(t, A_HEAD_DIM // 2, 1)
        return t * cos + jnp.where(first_half, up, dn) * sin

    q = rope(mm(C_AQ, C_AK)) * (A_HEAD_DIM ** -0.5)
    k = rope(mm(C_AK, C_AV))
    aq_ref[...] = q
    ak_ref[...] = k.astype(BF16)
    km_ref[0] = jnp.mean(k, axis=0, keepdims=True)

    v = mm(C_AV, C_MQK)
    lane128 = _iota((tm, LANES), 1)
    low = lane128 < A_HEAD_DIM
    for p in range(A_HEADS // 2):
        vp = v[:, p * LANES:(p + 1) * LANES]
        av_ref[:, (2 * p) * LANES:(2 * p + 1) * LANES] = jnp.where(low, vp, 1.0).astype(BF16)
        av_ref[:, (2 * p + 1) * LANES:(2 * p + 2) * LANES] = jnp.where(
            low, pltpu.roll(vp, A_HEAD_DIM, 1), 1.0).astype(BF16)

    mqk_ref[...] = mm(C_MQK, C_MV)
    mv_ref[...] = mm(C_MV, C_MO).astype(BF16)
    mo_ref[...] = mm(C_MO, C_GAM)
    gam_ref[...] = mm(C_GAM, C_GIF)
    gif_ref[...] = mm(C_GIF, C_END) + gb_ref[...]


def _inproj(x2, nw, w_main, gate_bias, cos_t, sin_t, seq):
    t = x2.shape[0]
    tm = ROW_TILE
    assert seq % tm == 0 and tm == MOBA_BLOCK
    nsteps = t // tm
    spb = seq // tm
    row = lambda w: pl.BlockSpec((tm, w), lambda i: (i, 0))
    full = lambda a: pl.BlockSpec(a.shape, lambda i: (0,) * a.ndim)
    tab = pl.BlockSpec((tm, LANES), lambda i: (i % spb, 0))
    out_shapes = (
        jax.ShapeDtypeStruct((t, A_WIDTH), F32),
        jax.ShapeDtypeStruct((t, A_WIDTH), BF16),
        jax.ShapeDtypeStruct((t, 2 * A_WIDTH), BF16),
        jax.ShapeDtypeStruct((nsteps, 1, A_WIDTH), F32),
        jax.ShapeDtypeStruct((t, 2 * M_WIDTH), F32),
        jax.ShapeDtypeStruct((t, M_WIDTH), BF16),
        jax.ShapeDtypeStruct((t, M_WIDTH), F32),
        jax.ShapeDtypeStruct((t, 2 * D_MODEL), F32),
        jax.ShapeDtypeStruct((t, LANES), F32),
    )
    out_specs = (row(A_WIDTH), row(A_WIDTH), row(2 * A_WIDTH),
                 pl.BlockSpec((1, 1, A_WIDTH), lambda i: (i, 0, 0)),
                 row(2 * M_WIDTH), row(M_WIDTH), row(M_WIDTH), row(2 * D_MODEL), row(LANES))
    vmem = 2 * (w_main.size * 2 + tm * D_MODEL * 4 + tm * C_END * 4) + 8 * tm * C_END
    return pl.pallas_call(
        _inproj_kernel, grid=(nsteps,),
        in_specs=[row(D_MODEL), full(nw), full(w_main), full(gate_bias), tab, tab],
        out_specs=out_specs, out_shape=out_shapes,
        compiler_params=_params(vmem), name="inproj",
    )(x2, nw, w_main, gate_bias, cos_t, sin_t)


def _attn_kernel(q_ref, k_ref, v_ref, km_ref, o_ref, m_sc, acc_sc):
    qi = pl.program_id(1)
    qc = q_ref.shape[1]
    nb = km_ref.shape[1]
    qf = q_ref[0]
    km = km_ref[0]
    kmt = jnp.concatenate([km] * A_HEADS, axis=0)
    r = _iota(kmt.shape, 0)
    c = _iota(kmt.shape, 1)
    kmt = jnp.where((r // nb) == (c // A_HEAD_DIM), kmt, 0.0)
    gate_t = _nt_dot(kmt, qf, precision=HIGHEST)

    blk = _iota((nb, qc), 0).astype(F32)
    past = _iota((nb, qc), 0) < qi
    bias_rows = []
    for h in range(A_HEADS):
        g = jnp.where(past, gate_t[h * nb:(h + 1) * nb, :], NEG_INF)
        sel = jnp.zeros((nb, qc), F32)
        for _ in range(MOBA_TOPK):
            top = jnp.max(g, axis=0, keepdims=True)
            first = jnp.min(jnp.where(g == top, blk, float(nb)), axis=0, keepdims=True)
            hit = jnp.logical_and(blk == first, top > NEG_INF)
            sel = jnp.where(hit, 1.0, sel)
            g = jnp.where(hit, NEG_INF, g)
        bias_rows.append(jnp.where(sel > 0.0, 0.0, NEG))
    if A_HEADS * nb < LANES:
        bias_rows.append(jnp.zeros((LANES - A_HEADS * nb, qc), F32))
    bias = jnp.concatenate(bias_rows, axis=0).T

    lane = _iota((qc, LANES), 1)
    klane = _iota((MOBA_BLOCK, LANES), 1)
    causal = _iota((qc, MOBA_BLOCK), 1) <= _iota((qc, MOBA_BLOCK), 0)
    own = pl.multiple_of(qi * MOBA_BLOCK, MOBA_BLOCK)

    for p in range(A_HEADS // 2):
        ksl = slice(p * LANES, (p + 1) * LANES)
        qp = qf[:, ksl]
        q_aug = []
        for hh in range(2):
            h = 2 * p + hh
            qh = jnp.where((lane // A_HEAD_DIM) == hh, qp, 0.0).astype(BF16)
            bh = jnp.where((lane // nb) == h, bias, 0.0).astype(BF16)
            q_aug.append(jnp.concatenate([qh, bh], axis=1))
            s = _nt_dot(qh, k_ref[0, pl.ds(own, MOBA_BLOCK), ksl])
            s = jnp.where(causal, s, NEG)
            m0 = jnp.max(s, axis=1, keepdims=True)
            pr = jnp.exp(s - m0)
            m_sc[hh] = jnp.broadcast_to(m0, (qc, LANES))
            acc_sc[hh] = jnp.dot(pr.astype(BF16), v_ref[0, pl.ds(own, MOBA_BLOCK), h * LANES:(h + 1) * LANES],
                                 preferred_element_type=F32)

        def body(n, carry, p=p, ksl=ksl, q_aug=q_aug):
            start = pl.multiple_of(n * MOBA_BLOCK, MOBA_BLOCK)
            onehot = jnp.where((klane % nb) == n, 1.0, 0.0).astype(BF16)
            k_aug = jnp.concatenate([k_ref[0, pl.ds(start, MOBA_BLOCK), ksl], onehot], axis=1)
            for hh in range(2):
                h = 2 * p + hh
                s = _nt_dot(q_aug[hh], k_aug)
                m_prev = m_sc[hh]
                m_new = jnp.maximum(m_prev, jnp.max(s, axis=1, keepdims=True))
                alpha = jnp.exp(m_prev - m_new)
                pr = jnp.exp(s - jnp.concatenate([m_new, m_new], axis=1))
                pv = jnp.dot(pr.astype(BF16), v_ref[0, pl.ds(start, MOBA_BLOCK), h * LANES:(h + 1) * LANES],
                             preferred_element_type=F32)
                acc_sc[hh] = alpha * acc_sc[hh] + pv
                m_sc[hh] = m_new
            return carry

        lax.fori_loop(0, qi, body, 0)

        a0 = acc_sc[0]
        a1 = acc_sc[1]
        o0 = a0 / pltpu.roll(a0, A_HEAD_DIM, 1)
        o1 = a1 / pltpu.roll(a1, A_HEAD_DIM, 1)
        o_ref[0, :, ksl] = jnp.where(lane < A_HEAD_DIM, o0, pltpu.roll(o1, A_HEAD_DIM, 1)).astype(BF16)


def _attn(q, k, v, km):
    b, s, _ = q.shape
    nb = s // MOBA_BLOCK
    assert nb * A_HEADS <= LANES, "block-bias columns must fit one lane group"
    qc = MOBA_BLOCK
    vmem = 2 * (s * A_WIDTH * 2 + s * 2 * A_WIDTH * 2 + qc * A_WIDTH * 6) + 16 * qc * 256 * 4 + (4 << 20)
    return pl.pallas_call(
        _attn_kernel, grid=(b, s // qc),
        in_specs=[pl.BlockSpec((1, qc, A_WIDTH), lambda bi, i: (bi, i, 0)),
                  pl.BlockSpec((1, s, A_WIDTH), lambda bi, i: (bi, 0, 0)),
                  pl.BlockSpec((1, s, 2 * A_WIDTH), lambda bi, i: (bi, 0, 0)),
                  pl.BlockSpec((1, nb, A_WIDTH), lambda bi, i: (bi, 0, 0))],
        out_specs=pl.BlockSpec((1, qc, A_WIDTH), lambda bi, i: (bi, i, 0)),
        out_shape=jax.ShapeDtypeStruct((b, s, A_WIDTH), BF16),
        scratch_shapes=[pltpu.VMEM((2, qc, LANES), F32), pltpu.VMEM((2, qc, LANES), F32)],
        compiler_params=_params(vmem, 2), name="moba_attn",
    )(q, k, v, km)


def _mlstm_kernel(qk_ref, v_ref, o_ref, g_ref, cw_ref, cb_ref, hw_ref, y_ref, ext_sc, c_sc, m_sc):
    ci = pl.program_id(1)
    ln = qk_ref.shape[1]
    dh = M_HEAD_DIM

    @pl.when(ci == 0)
    def _():
        ext_sc[0:SUBLANES, :] = jnp.zeros((SUBLANES, 2 * M_WIDTH), F32)
        c_sc[...] = jnp.zeros(c_sc.shape, F32)
        m_sc[...] = jnp.zeros(m_sc.shape, F32)

    u = qk_ref[0]
    ext_sc[SUBLANES:SUBLANES + ln, :] = u
    conv = cb_ref[...]
    for j in range(M_CONV):
        conv = conv + cw_ref[j:j + 1, :] * ext_sc[pl.ds(SUBLANES - (M_CONV - 1) + j, ln), :]
    ext_sc[0:SUBLANES, :] = u[ln - SUBLANES:ln, :]
    act = conv * _sigmoid(conv)

    gates = g_ref[0]
    log_f = jnp.minimum(gates, 0.0) - jnp.log(1.0 + jnp.exp(-jnp.abs(gates)))
    row = _iota((ln, ln), 0)
    col = _iota((ln, ln), 1)
    causal = col <= row
    b_cols = jnp.dot(causal.astype(F32), log_f, precision=HIGHEST, preferred_element_type=F32)
    gates_t = gates.T
    b_rows = b_cols.T
    ones = jnp.ones((ln, dh), BF16)

    for h in range(M_HEADS):
        hs = slice(h * dh, (h + 1) * dh)
        qh = act[:, hs].astype(BF16)
        kh = act[:, M_WIDTH + h * dh:M_WIDTH + (h + 1) * dh] * (dh ** -0.5)
        b_col = b_cols[:, M_HEADS + h:M_HEADS + h + 1]
        b_row = b_rows[M_HEADS + h:M_HEADS + h + 1, :]
        i_col = gates[:, h:h + 1]
        i_row = gates_t[h:h + 1, :]
        b_last = b_col[ln - 1:ln, :]
        m_st = m_sc[h][0:1, 0:1]

        d = jnp.where(causal, b_col - b_row + i_row, NEG_INF)
        g = b_col + m_st
        m_t = jnp.maximum(g, jnp.max(d, axis=1, keepdims=True))
        w_intra = jnp.exp(d - m_t)
        w_inter = jnp.exp(g - m_t)
        qk = (_nt_dot(qh, kh.astype(BF16)) * w_intra).astype(BF16)
        v_aug = jnp.concatenate([v_ref[0, :, hs], ones], axis=1)
        c_aug = c_sc[h]
        res = (w_inter * jnp.dot(qh, c_aug.astype(BF16), preferred_element_type=F32)
               + jnp.dot(qk, v_aug, preferred_element_type=F32))
        num = res[:, :dh]
        den = res[:, dh:]
        h_t = num / jnp.maximum(jnp.abs(den), jnp.exp(-m_t))

        m_new = jnp.maximum(b_last + m_st, jnp.max(b_last - b_row + i_row, axis=1, keepdims=True))
        w_k = jnp.exp(b_last - b_col + i_col - m_new)
        decay = jnp.exp(b_last + m_st - m_new)
        kw_t = (kh * w_k).T.astype(BF16)
        c_sc[h] = decay * c_aug + jnp.dot(kw_t, v_aug, preferred_element_type=F32)
        m_sc[h] = jnp.broadcast_to(m_new, (SUBLANES, LANES))

        hn = h_t * lax.rsqrt(jnp.mean(h_t * h_t, axis=1, keepdims=True) + EPS) * hw_ref[:, hs]
        y_ref[0, :, hs] = (hn * _sigmoid(o_ref[0, :, hs])).astype(BF16)


def _mlstm(mqk, mv, mo, gif, conv_w, conv_b, head_w):
    b, s, _ = mqk.shape
    ln = MLSTM_CHUNK
    assert s % ln == 0
    blk = lambda w: pl.BlockSpec((1, ln, w), lambda bi, i: (bi, i, 0))
    full = lambda a: pl.BlockSpec(a.shape, lambda bi, i: (0,) * a.ndim)
    vmem = 2 * ln * (2 * M_WIDTH * 4 + M_WIDTH * 10 + LANES * 4) + 24 * ln * ln * 4 + (8 << 20)
    return pl.pallas_call(
        _mlstm_kernel, grid=(b, s // ln),
        in_specs=[blk(2 * M_WIDTH), blk(M_WIDTH), blk(M_WIDTH), blk(LANES),
                  full(conv_w), full(conv_b), full(head_w)],
        out_specs=blk(M_WIDTH),
        out_shape=jax.ShapeDtypeStruct((b, s, M_WIDTH), BF16),
        scratch_shapes=[pltpu.VMEM((ln + SUBLANES, 2 * M_WIDTH), F32),
                        pltpu.VMEM((M_HEADS, M_HEAD_DIM, 2 * M_HEAD_DIM), F32),
                        pltpu.VMEM((M_HEADS, SUBLANES, LANES), F32)],
        compiler_params=_params(vmem, 2), name="mlstm",
    )(mqk, mv, mo, gif, conv_w, conv_b, head_w)


def _merge_kernel(ya_ref, ym_ref, gam_ref, x_ref, wb_ref, wo_ref, nw_ref, wr_ref, br_ref,
                  h_ref, xn_ref, rmeta_ref, gmeta_ref, cnt_ref, cnt_sc):
    i = pl.program_id(0)
    tm = x_ref.shape[0]

    @pl.when(i == 0)
    def _():
        cnt_sc[...] = jnp.zeros(cnt_sc.shape, F32)

    pa = jnp.dot(ya_ref[...], wb_ref[0:A_WIDTH, :], preferred_element_type=F32)
    pm = jnp.dot(ym_ref[...], wb_ref[A_WIDTH:, :], preferred_element_type=F32)
    merged = _sigmoid(gam_ref[:, 0:D_MODEL]) * pa + _sigmoid(gam_ref[:, D_MODEL:]) * pm
    h = x_ref[...] + jnp.dot(merged.astype(BF16), wo_ref[...], preferred_element_type=F32)
    h_ref[...] = h
    xn = h * lax.rsqrt(jnp.mean(h * h, axis=-1, keepdims=True) + EPS) * nw_ref[...]
    xn_ref[...] = xn

    logits = jnp.dot(xn, wr_ref[...], precision=HIGHEST, preferred_element_type=F32) + br_ref[...]
    lane = _iota((tm, LANES), 1)
    lane_f = lane.astype(F32)
    cur = logits
    vals, idxs, hits = [], [], []
    for _ in range(TOP_K):
        top = jnp.max(cur, axis=1, keepdims=True)
        first = jnp.min(jnp.where(cur == top, lane_f, float(LANES)), axis=1, keepdims=True)
        hit = lane_f == first
        vals.append(top)
        idxs.append(first)
        hits.append(hit)
        cur = jnp.where(hit, NEG_INF, cur)
    exps = [jnp.exp(v - vals[0]) for v in vals]
    inv = 1.0 / (exps[0] + exps[1] + exps[2] + exps[3])

    onehot = jnp.zeros((tm, LANES), F32)
    for hit in hits:
        onehot = jnp.where(hit, 1.0, onehot)
    before = (_iota((tm, tm), 1) < _iota((tm, tm), 0)).astype(BF16)
    seen = jnp.dot(before, onehot.astype(BF16), preferred_element_type=F32) + cnt_sc[0:1, :]
    rmeta = jnp.zeros((tm, LANES), F32)
    gmeta = jnp.zeros((tm, LANES), F32)
    for kk in range(TOP_K):
        rank = jnp.sum(jnp.where(hits[kk], seen, 0.0), axis=1, keepdims=True)
        rmeta = jnp.where(lane == kk, idxs[kk], rmeta)
        rmeta = jnp.where(lane == TOP_K + kk, rank, rmeta)
        gmeta = jnp.where(lane == kk, exps[kk] * inv, gmeta)
    rmeta_ref[...] = rmeta.astype(I32)
    gmeta_ref[...] = gmeta
    cnt_sc[...] = cnt_sc[...] + jnp.sum(onehot, axis=0, keepdims=True)
    cnt_ref[...] = cnt_sc[...]


def _merge(ya, ym, gam, x2, wb, wo, nw, wr, br):
    t = x2.shape[0]
    tm = ROW_TILE
    row = lambda w: pl.BlockSpec((tm, w), lambda i: (i, 0))
    full = lambda a: pl.BlockSpec(a.shape, lambda i: (0,) * a.ndim)
    out_shapes = (jax.ShapeDtypeStruct((t, D_MODEL), F32), jax.ShapeDtypeStruct((t, D_MODEL), F32),
                  jax.ShapeDtypeStruct((t, LANES), I32), jax.ShapeDtypeStruct((t, LANES), F32),
                  jax.ShapeDtypeStruct((SUBLANES, LANES), F32))
    vmem = 2 * (wb.size * 2 + wo.size * 2 + wr.size * 4 + tm * D_MODEL * 22) + 16 * tm * D_MODEL * 4
    return pl.pallas_call(
        _merge_kernel, grid=(t // tm,),
        in_specs=[row(A_WIDTH), row(M_WIDTH), row(2 * D_MODEL), row(D_MODEL),
                  full(wb), full(wo), full(nw), full(wr), full(br)],
        out_specs=(row(D_MODEL), row(D_MODEL), row(LANES), row(LANES),
                   pl.BlockSpec((SUBLANES, LANES), lambda i: (0, 0))),
        out_shape=out_shapes,
        scratch_shapes=[pltpu.VMEM((SUBLANES, LANES), F32)],
        compiler_params=_params(vmem), name="merge_route",
    )(ya, ym, gam, x2, wb, wo, nw, wr, br)


def _row_copy(src_ref, dst_ref, src_row, dst_row, n, sem):
    return pltpu.make_async_copy(src_ref.at[pl.ds(src_row, n)], dst_ref.at[pl.ds(dst_row, n)], sem)


def _push_kernel(tail_start_ref, tail_n_ref, n_used_ref, dest_ref, xn_ref, rows_ref, zero_sc, sem, zsem):
    i = pl.program_id(0)
    tm = xn_ref.shape[0]

    def start(t, carry):
        for kk in range(TOP_K):
            _row_copy(xn_ref, rows_ref, t, dest_ref[t * TOP_K + kk], 1, sem).start()
        return carry

    lax.fori_loop(0, tm, start, 0)

    @pl.when(i == 0)
    def _():
        zero_sc[...] = jnp.zeros(zero_sc.shape, F32)

        def fill(e, carry):
            base = tail_start_ref[e]

            def zstart(r, c):
                _row_copy(zero_sc, rows_ref, 0, base + r, 1, zsem).start()
                return c

            def zwait(r, c):
                _row_copy(zero_sc, rows_ref, 0, base + r, 1, zsem).wait()
                return c

            lax.fori_loop(0, tail_n_ref[e], zstart, 0)
            lax.fori_loop(0, tail_n_ref[e], zwait, 0)
            return carry

        lax.fori_loop(0, N_EXPERTS, fill, 0)

        def zblock(j, carry):
            cp = _row_copy(zero_sc, rows_ref, 0, pl.multiple_of(j * MOE_BLOCK, MOE_BLOCK), MOE_BLOCK, zsem)
            cp.start()
            cp.wait()
            return carry

        lax.fori_loop(n_used_ref[0], rows_ref.shape[0] // MOE_BLOCK, zblock, 0)

    def wait(t, carry):
        for kk in range(TOP_K):
            _row_copy(xn_ref, rows_ref, t, dest_ref[t * TOP_K + kk], 1, sem).wait()
        return carry

    lax.fori_loop(0, tm, wait, 0)


def _push(xn2, dest, tail_start, tail_n, n_used, n_rows):
    t = xn2.shape[0]
    tm = ROW_TILE
    grid_spec = pltpu.PrefetchScalarGridSpec(
        num_scalar_prefetch=3, grid=(t // tm,),
        in_specs=[pl.BlockSpec((tm * TOP_K,), lambda i, *_: (i,), memory_space=pltpu.SMEM),
                  pl.BlockSpec((tm, D_MODEL), lambda i, *_: (i, 0))],
        out_specs=pl.BlockSpec(memory_space=pl.ANY),
        scratch_shapes=[pltpu.VMEM((MOE_BLOCK, D_MODEL), F32),
                        pltpu.SemaphoreType.DMA(()), pltpu.SemaphoreType.DMA(())])
    return pl.pallas_call(
        _push_kernel, grid_spec=grid_spec,
        out_shape=jax.ShapeDtypeStruct((n_rows, D_MODEL), F32),
        compiler_params=_params(6 * tm * D_MODEL * 4 + (4 << 20)), name="moe_push",
    )(tail_start, tail_n, n_used, dest, xn2)


def _expert_kernel(blk_e_ref, n_used_ref, x_ref, w1_ref, b1_ref, w2_ref, b2_ref, y_ref, w1_sc, w2_sc):
    i = pl.program_id(0)
    prev = blk_e_ref[jnp.maximum(i - 1, 0)]
    fresh = jnp.logical_or(i == 0, blk_e_ref[i] != prev)

    @pl.when(jnp.logical_and(fresh, i < n_used_ref[0]))
    def _():
        w1_sc[...] = w1_ref[0].astype(BF16)
        w2_sc[...] = w2_ref[0].astype(BF16)

    @pl.when(i < n_used_ref[0])
    def _():
        xb = x_ref[...].astype(BF16)
        hdn = jnp.dot(xb, w1_sc[...], preferred_element_type=F32) + b1_ref[0]
        glu = jnp.minimum(hdn[:, :D_FF], SWIGLU_LIMIT)
        lin = jnp.clip(hdn[:, D_FF:], -SWIGLU_LIMIT, SWIGLU_LIMIT)
        act = glu * _sigmoid(SWIGLU_ALPHA * glu) * (lin + 1.0)
        y_ref[...] = jnp.dot(act.astype(BF16), w2_sc[...], preferred_element_type=F32) + b2_ref[0]

    @pl.when(i >= n_used_ref[0])
    def _():
        y_ref[...] = jnp.zeros(y_ref.shape, F32)


def _expert(x_rows, blk_expert, n_used, w1, b1, w2, b2):
    n_rows = x_rows.shape[0]
    n_blk = n_rows // MOE_BLOCK
    blk = lambda i, be, nu: (jnp.minimum(i, nu[0] - 1), 0)
    wsel = lambda i, be, nu: (be[i], 0, 0)
    grid_spec = pltpu.PrefetchScalarGridSpec(
        num_scalar_prefetch=2, grid=(n_blk,),
        in_specs=[pl.BlockSpec((MOE_BLOCK, D_MODEL), blk),
                  pl.BlockSpec((1, D_MODEL, 2 * D_FF), wsel),
                  pl.BlockSpec((1, 1, 2 * D_FF), wsel),
                  pl.BlockSpec((1, D_FF, D_MODEL), wsel),
                  pl.BlockSpec((1, 1, D_MODEL), wsel)],
        out_specs=pl.BlockSpec((MOE_BLOCK, D_MODEL), lambda i, be, nu: (i, 0)),
        scratch_shapes=[pltpu.VMEM((D_MODEL, 2 * D_FF), BF16), pltpu.VMEM((D_FF, D_MODEL), BF16)])
    vmem = 2 * (D_MODEL * 2 * D_FF * 4 + D_FF * D_MODEL * 4) + 3 * D_MODEL * D_FF * 2 + 12 * MOE_BLOCK * D_MODEL * 4
    return pl.pallas_call(
        _expert_kernel, grid_spec=grid_spec,
        out_shape=jax.ShapeDtypeStruct((n_rows, D_MODEL), F32),
        compiler_params=_params(vmem), name="moe_expert",
    )(blk_expert, n_used, x_rows, w1, b1, w2, b2)


def _final_kernel(dest_ref, h_ref, g_ref, nw_ref, rows_ref, o_ref, buf, sem):
    tm = h_ref.shape[0]

    def start(t, carry):
        for kk in range(TOP_K):
            _row_copy(rows_ref, buf.at[kk], dest_ref[t * TOP_K + kk], t, 1, sem).start()
        return carry

    lax.fori_loop(0, tm, start, 0)

    def wait(t, carry):
        for kk in range(TOP_K):
            _row_copy(rows_ref, buf.at[kk], dest_ref[t * TOP_K + kk], t, 1, sem).wait()
        return carry

    lax.fori_loop(0, tm, wait, 0)

    gates = g_ref[...]
    y = h_ref[...]
    for kk in range(TOP_K):
        y = y + gates[:, kk:kk + 1] * buf[kk]
    o_ref[...] = y * lax.rsqrt(jnp.mean(y * y, axis=-1, keepdims=True) + EPS) * nw_ref[...]


def _final(h, gmeta, nw, y_rows, dest):
    t = h.shape[0]
    tm = ROW_TILE
    return pl.pallas_call(
        _final_kernel, grid=(t // tm,),
        in_specs=[pl.BlockSpec((tm * TOP_K,), lambda i: (i,), memory_space=pltpu.SMEM),
                  pl.BlockSpec((tm, D_MODEL), lambda i: (i, 0)),
                  pl.BlockSpec((tm, LANES), lambda i: (i, 0)),
                  pl.BlockSpec(nw.shape, lambda i: (0, 0)),
                  pl.BlockSpec(memory_space=pl.ANY)],
        out_specs=pl.BlockSpec((tm, D_MODEL), lambda i: (i, 0)),
        out_shape=jax.ShapeDtypeStruct((t, D_MODEL), F32),
        scratch_shapes=[pltpu.VMEM((TOP_K, tm, D_MODEL), F32), pltpu.SemaphoreType.DMA(())],
        compiler_params=_params(10 * tm * D_MODEL * 4 + (4 << 20)), name="moe_combine",
    )(dest, h, gmeta, nw, y_rows)


def _rope_tables(seq):
    half = A_HEAD_DIM // 2
    inv = ROPE_THETA ** (-jnp.arange(0, A_HEAD_DIM, 2, dtype=F32) / A_HEAD_DIM)
    ang = jnp.arange(seq, dtype=F32)[:, None] * inv[None, :]
    cos = jnp.concatenate([jnp.cos(ang)] * 4, axis=-1)
    sin = jnp.sin(ang)
    sin = jnp.concatenate([-sin, sin, -sin, sin], axis=-1)
    del half
    return cos, sin


def _route_plan(rmeta, cnt, n_tok):
    idx = rmeta[:, 0:TOP_K]
    rank = rmeta[:, TOP_K:2 * TOP_K]
    counts = cnt[0, :N_EXPERTS].astype(I32)
    padded = (counts + MOE_BLOCK - 1) // MOE_BLOCK * MOE_BLOCK
    pad_end = jnp.cumsum(padded)
    pad_start = pad_end - padded
    dest = (pad_start[idx] + rank).reshape(n_tok * TOP_K).astype(I32)
    n_blk = (n_tok * TOP_K) // MOE_BLOCK + N_EXPERTS
    blk_expert = jnp.minimum(
        jnp.searchsorted(pad_end, jnp.arange(n_blk, dtype=I32) * MOE_BLOCK, side="right"),
        N_EXPERTS - 1).astype(I32)
    n_used = (pad_end[-1:] // MOE_BLOCK).astype(I32)
    return dest, blk_expert, n_used, (pad_start + counts).astype(I32), (padded - counts).astype(I32), n_blk


def _layer(h3, norm_mix_w, w_in, m_conv_w, m_conv_b, m_gate_bias, m_head_norm_w, w_branch, w_out,
           norm_ffn_w, w_router, b_router, w_mlp1, b_mlp1, w_mlp2, b_mlp2, norm_out_w):
    b, s, d = h3.shape
    t = b * s
    x2 = h3.reshape(t, d)
    c_if = 3 * A_WIDTH + 4 * M_WIDTH
    w_main = jnp.concatenate(
        [w_in[:, :c_if], w_in[:, c_if + 2 * M_HEADS:],
         jnp.pad(w_in[:, c_if:c_if + 2 * M_HEADS], ((0, 0), (0, LANES - 2 * M_HEADS)))], axis=1).astype(BF16)
    gate_bias = jnp.pad(m_gate_bias, (0, LANES - 2 * M_HEADS)).reshape(1, LANES)
    cos_t, sin_t = _rope_tables(s)

    aq, ak, av, km, mqk, mv, mo, gam, gif = _inproj(
        x2, norm_mix_w.reshape(1, d), w_main, gate_bias, cos_t, sin_t, s)

    ya = _attn(aq.reshape(b, s, A_WIDTH), ak.reshape(b, s, A_WIDTH), av.reshape(b, s, 2 * A_WIDTH),
               km.reshape(b, s // MOBA_BLOCK, A_WIDTH))
    ym = _mlstm(mqk.reshape(b, s, 2 * M_WIDTH), mv.reshape(b, s, M_WIDTH), mo.reshape(b, s, M_WIDTH),
                gif.reshape(b, s, LANES), m_conv_w, m_conv_b.reshape(1, -1), m_head_norm_w.reshape(1, -1))

    wr = jnp.pad(w_router, ((0, 0), (0, LANES - N_EXPERTS)))
    br = jnp.concatenate([b_router, jnp.full((LANES - N_EXPERTS,), NEG, F32)]).reshape(1, LANES)
    h, xn2, rmeta, gmeta, cnt = _merge(
        ya.reshape(t, A_WIDTH), ym.reshape(t, M_WIDTH), gam, x2, w_branch.astype(BF16), w_out.astype(BF16),
        norm_ffn_w.reshape(1, d), wr, br)

    dest, blk_expert, n_used, tail_start, tail_n, n_blk = _route_plan(rmeta, cnt, t)
    x_rows = _push(xn2, dest, tail_start, tail_n, n_used, n_blk * MOE_BLOCK)
    y_rows = _expert(x_rows, blk_expert, n_used, w_mlp1, b_mlp1.reshape(N_EXPERTS, 1, -1),
                     w_mlp2, b_mlp2.reshape(N_EXPERTS, 1, -1))
    out = _final(h, gmeta, norm_out_w.reshape(1, d), y_rows, dest)
    return out.reshape(b, s, d)


def kernel(x, norm_mix_w, w_in, m_conv_w, m_conv_b, m_gate_bias, m_head_norm_w, w_branch, w_out,
           norm_ffn_w, w_router, b_router, w_mlp1, b_mlp1, w_mlp2, b_mlp2, norm_final_w):
    depth = norm_mix_w.shape[0]
    assert depth == 1, "the final RMSNorm is fused into the layer's last kernel"
    return _layer(x, norm_mix_w[0], w_in[0], m_conv_w[0], m_conv_b[0], m_gate_bias[0], m_head_norm_w[0],
                  w_branch[0], w_out[0], norm_ffn_w[0], w_router[0], b_router[0], w_mlp1[0], b_mlp1[0],
                  w_mlp2[0], b_mlp2[0], norm_final_w)
```

```python
import functools

import jax
import jax.numpy as jnp
import numpy as np
from jax import lax
from jax.experimental import pallas as pl
from jax.experimental.pallas import tpu as pltpu

F32 = jnp.float32
BF16 = jnp.bfloat16
I32 = jnp.int32
HIGHEST = lax.Precision.HIGHEST

D_MODEL = 1024
A_HEADS = 8
A_HEAD_DIM = 64
A_WIDTH = A_HEADS * A_HEAD_DIM
MOBA_BLOCK = 256
MOBA_TOPK = 3
M_HEADS = 4
M_HEAD_DIM = 128
M_WIDTH = M_HEADS * M_HEAD_DIM
M_CONV = 4
N_EXPERTS = 32
TOP_K = 4
D_FF = 1024
SWIGLU_LIMIT = 7.0
SWIGLU_ALPHA = 1.702
MOE_BLOCK = 256
ROPE_THETA = 10000.0
EPS = 1e-6
NEG = -1e30
NEG_INF = float("-inf")

LANES = 128
SUBLANES = 8
VMEM_LIMIT_CAP = 56 * 1024 * 1024

C_AQ, C_AK, C_AV = 0, 512, 1024
C_MQK, C_MV, C_MO = 1536, 2560, 3072
C_GAM, C_GIF, C_END = 3584, 5632, 5760

ROW_TILE = 256
Q_SCALE = (A_HEAD_DIM ** -0.5) * 1.4426950408889634
MLSTM_CHUNK = 256


def _params(vmem_bytes, n_axes=1):
    return pltpu.CompilerParams(
        dimension_semantics=("arbitrary",) * n_axes,
        vmem_limit_bytes=int(min(max(vmem_bytes, 16 * 1024 * 1024), VMEM_LIMIT_CAP)))


def _iota(shape, dim):
    return lax.broadcasted_iota(I32, shape, dim)


def _sigmoid(x):
    return 1.0 / (1.0 + jnp.exp(-x))


def _nt_dot(a, b, precision=None):
    return lax.dot_general(a, b, (((1,), (1,)), ((), ())), precision=precision,
                           preferred_element_type=F32)


def _inproj_kernel(x_ref, nw_ref, w_ref, gb_ref, cos_ref, sin_ref,
                   aq_ref, ak_ref, av_ref, km_ref, mqk_ref, mv_ref, mo_ref, gam_ref, gif_ref):
    tm = x_ref.shape[0]
    x = x_ref[...]
    xn = x * lax.rsqrt(jnp.mean(x * x, axis=-1, keepdims=True) + EPS) * nw_ref[...]
    xb = xn.astype(BF16)

    def mm(lo, hi):
        return jnp.dot(xb, w_ref[:, lo:hi], preferred_element_type=F32)

    cos = jnp.concatenate([cos_ref[...]] * 4, axis=1)
    sin = jnp.concatenate([sin_ref[...]] * 4, axis=1)
    lane = _iota((tm, A_WIDTH), 1)
    first_half = (lane & (A_HEAD_DIM - 1)) < (A_HEAD_DIM // 2)

    def rope(t):
        up = pltpu.roll(t, A_WIDTH - A_HEAD_DIM // 2, 1)
        dn = pltpu.roll(t, A_HEAD_DIM // 2, 1)
        return t * cos + jnp.where(first_half, up, dn) * sin

    q = rope(mm(C_AQ, C_AK)) * Q_SCALE
    k = rope(mm(C_AK, C_AV))
    aq_ref[...] = q
    ak_ref[...] = k.astype(BF16)
    km_ref[0] = jnp.mean(k, axis=0, keepdims=True)

    v = mm(C_AV, C_MQK)
    lane128 = _iota((tm, LANES), 1)
    low = lane128 < A_HEAD_DIM
    for p in range(A_HEADS // 2):
        vp = v[:, p * LANES:(p + 1) * LANES]
        av_ref[:, (2 * p) * LANES:(2 * p + 1) * LANES] = jnp.where(low, vp, 1.0).astype(BF16)
        av_ref[:, (2 * p + 1) * LANES:(2 * p + 2) * LANES] = jnp.where(
            low, pltpu.roll(vp, A_HEAD_DIM, 1), 1.0).astype(BF16)

    mqk_ref[...] = mm(C_MQK, C_MV)
    mv_ref[...] = mm(C_MV, C_MO).astype(BF16)
    mo_ref[...] = mm(C_MO, C_GAM)
    gam_ref[...] = mm(C_GAM, C_GIF)
    gif_ref[...] = mm(C_GIF, C_END) + gb_ref[...]


def _inproj(x2, nw, w_main, gate_bias, cos_t, sin_t, seq):
    t = x2.shape[0]
    tm = ROW_TILE
    assert seq % tm == 0 and tm == MOBA_BLOCK
    nsteps = t // tm
    spb = seq // tm
    row = lambda w: pl.BlockSpec((tm, w), lambda i: (i, 0))
    full = lambda a: pl.BlockSpec(a.shape, lambda i: (0,) * a.ndim)
    tab = pl.BlockSpec((tm, LANES), lambda i: (i % spb, 0))
    out_shapes = (
        jax.ShapeDtypeStruct((t, A_WIDTH), F32),
        jax.ShapeDtypeStruct((t, A_WIDTH), BF16),
        jax.ShapeDtypeStruct((t, 2 * A_WIDTH), BF16),
        jax.ShapeDtypeStruct((nsteps, 1, A_WIDTH), F32),
        jax.ShapeDtypeStruct((t, 2 * M_WIDTH), F32),
        jax.ShapeDtypeStruct((t, M_WIDTH), BF16),
        jax.ShapeDtypeStruct((t, M_WIDTH), F32),
        jax.ShapeDtypeStruct((t, 2 * D_MODEL), F32),
        jax.ShapeDtypeStruct((t, LANES), F32),
    )
    out_specs = (row(A_WIDTH), row(A_WIDTH), row(2 * A_WIDTH),
                 pl.BlockSpec((1, 1, A_WIDTH), lambda i: (i, 0, 0)),
                 row(2 * M_WIDTH), row(M_WIDTH), row(M_WIDTH), row(2 * D_MODEL), row(LANES))
    vmem = 2 * (w_main.size * 2 + tm * D_MODEL * 4 + tm * C_END * 4) + 8 * tm * C_END
    return pl.pallas_call(
        _inproj_kernel, grid=(nsteps,),
        in_specs=[row(D_MODEL), full(nw), full(w_main), full(gate_bias), tab, tab],
        out_specs=out_specs, out_shape=out_shapes,
        compiler_params=_params(vmem), name="inproj",
    )(x2, nw, w_main, gate_bias, cos_t, sin_t)


def _attn_kernel(q_ref, k_ref, v_ref, km_ref, o_ref, m_sc, acc_sc, qa_sc):
    qi = pl.program_id(1)
    qc = q_ref.shape[1]
    nb = km_ref.shape[1]
    qf = q_ref[0]
    km = km_ref[0]
    kmt = jnp.concatenate([km] * A_HEADS, axis=0)
    r = _iota(kmt.shape, 0)
    c = _iota(kmt.shape, 1)
    kmt = jnp.where((r // nb) == (c // A_HEAD_DIM), kmt, 0.0)
    gate_t = _nt_dot(kmt, qf, precision=HIGHEST)

    blk = _iota((nb, qc), 0).astype(F32)
    past = _iota((nb, qc), 0) < qi
    bias_rows = []
    for h in range(A_HEADS):
        g = jnp.where(past, gate_t[h * nb:(h + 1) * nb, :], NEG_INF)
        sel = jnp.zeros((nb, qc), F32)
        for _ in range(MOBA_TOPK):
            top = jnp.max(g, axis=0, keepdims=True)
            first = jnp.min(jnp.where(g == top, blk, float(nb)), axis=0, keepdims=True)
            hit = jnp.logical_and(blk == first, top > NEG_INF)
            sel = jnp.where(hit, 1.0, sel)
            g = jnp.where(hit, NEG_INF, g)
        bias_rows.append(jnp.where(sel > 0.0, 0.0, NEG))
    if A_HEADS * nb < LANES:
        bias_rows.append(jnp.zeros((LANES - A_HEADS * nb, qc), F32))
    bias = jnp.concatenate(bias_rows, axis=0).T

    lane = _iota((qc, LANES), 1)
    klane = _iota((MOBA_BLOCK, LANES), 1)
    causal = _iota((qc, MOBA_BLOCK), 1) <= _iota((qc, MOBA_BLOCK), 0)
    own = pl.multiple_of(qi * MOBA_BLOCK, MOBA_BLOCK)

    for h in range(A_HEADS):
        ksl = slice((h // 2) * LANES, (h // 2 + 1) * LANES)
        qh = jnp.where((lane // A_HEAD_DIM) == (h % 2), qf[:, ksl], 0.0).astype(BF16)
        bh = jnp.where((lane // nb) == h, bias, 0.0).astype(BF16)
        qa_sc[h] = jnp.concatenate([qh, bh], axis=1)
        s = jnp.where(causal, _nt_dot(qh, k_ref[0, pl.ds(own, MOBA_BLOCK), ksl]), NEG)
        m0 = jnp.max(s, axis=1, keepdims=True)
        pr = jnp.exp2(s - m0)
        m_sc[h] = jnp.broadcast_to(m0, (qc, LANES))
        acc_sc[h] = jnp.dot(pr.astype(BF16), v_ref[0, pl.ds(own, MOBA_BLOCK), h * LANES:(h + 1) * LANES],
                            preferred_element_type=F32)

    def body(n, carry):
        start = pl.multiple_of(n * MOBA_BLOCK, MOBA_BLOCK)
        onehot = jnp.where((klane % nb) == n, 1.0, 0.0).astype(BF16)
        for p in range(A_HEADS // 2):
            k_aug = jnp.concatenate([k_ref[0, pl.ds(start, MOBA_BLOCK), p * LANES:(p + 1) * LANES], onehot], axis=1)
            for h in (2 * p, 2 * p + 1):
                s = _nt_dot(qa_sc[h], k_aug)
                m_prev = m_sc[h]
                m_new = jnp.maximum(m_prev, jnp.max(s, axis=1, keepdims=True))
                alpha = jnp.exp2(m_prev - m_new)
                pr = jnp.exp2(s - jnp.concatenate([m_new, m_new], axis=1))
                pv = jnp.dot(pr.astype(BF16), v_ref[0, pl.ds(start, MOBA_BLOCK), h * LANES:(h + 1) * LANES],
                             preferred_element_type=F32)
                acc_sc[h] = alpha * acc_sc[h] + pv
                m_sc[h] = m_new
        return carry

    lax.fori_loop(0, qi, body, 0)

    for p in range(A_HEADS // 2):
        a0 = acc_sc[2 * p]
        a1 = acc_sc[2 * p + 1]
        o0 = a0 / pltpu.roll(a0, A_HEAD_DIM, 1)
        o1 = a1 / pltpu.roll(a1, A_HEAD_DIM, 1)
        o_ref[0, :, p * LANES:(p + 1) * LANES] = jnp.where(
            lane < A_HEAD_DIM, o0, pltpu.roll(o1, A_HEAD_DIM, 1)).astype(BF16)


def _attn(q, k, v, km):
    b, s, _ = q.shape
    nb = s // MOBA_BLOCK
    assert nb * A_HEADS <= LANES, "block-bias columns must fit one lane group"
    qc = MOBA_BLOCK
    vmem = 2 * (s * A_WIDTH * 2 + s * 2 * A_WIDTH * 2 + qc * A_WIDTH * 6) + 16 * qc * 256 * 4 + (4 << 20)
    return pl.pallas_call(
        _attn_kernel, grid=(b, s // qc),
        in_specs=[pl.BlockSpec((1, qc, A_WIDTH), lambda bi, i: (bi, i, 0)),
                  pl.BlockSpec((1, s, A_WIDTH), lambda bi, i: (bi, 0, 0)),
                  pl.BlockSpec((1, s, 2 * A_WIDTH), lambda bi, i: (bi, 0, 0)),
                  pl.BlockSpec((1, nb, A_WIDTH), lambda bi, i: (bi, 0, 0))],
        out_specs=pl.BlockSpec((1, qc, A_WIDTH), lambda bi, i: (bi, i, 0)),
        out_shape=jax.ShapeDtypeStruct((b, s, A_WIDTH), BF16),
        scratch_shapes=[pltpu.VMEM((A_HEADS, qc, LANES), F32), pltpu.VMEM((A_HEADS, qc, LANES), F32),
                        pltpu.VMEM((A_HEADS, qc, 2 * LANES), BF16)],
        compiler_params=_params(vmem, 2), name="moba_attn",
    )(q, k, v, km)


def _mlstm_kernel(qk_ref, v_ref, o_ref, g_ref, cw_ref, cb_ref, hw_ref, y_ref, ext_sc, c_sc, m_sc):
    ci = pl.program_id(1)
    ln = qk_ref.shape[1]
    dh = M_HEAD_DIM

    @pl.when(ci == 0)
    def _():
        ext_sc[0:SUBLANES, :] = jnp.zeros((SUBLANES, 2 * M_WIDTH), F32)
        c_sc[...] = jnp.zeros(c_sc.shape, F32)
        m_sc[...] = jnp.zeros(m_sc.shape, F32)

    u = qk_ref[0]
    ext_sc[SUBLANES:SUBLANES + ln, :] = u
    conv = cb_ref[...]
    for j in range(M_CONV):
        conv = conv + cw_ref[j:j + 1, :] * ext_sc[pl.ds(SUBLANES - (M_CONV - 1) + j, ln), :]
    ext_sc[0:SUBLANES, :] = u[ln - SUBLANES:ln, :]
    act = conv * _sigmoid(conv)

    gates = g_ref[0]
    log_f = jnp.minimum(gates, 0.0) - jnp.log(1.0 + jnp.exp(-jnp.abs(gates)))
    row = _iota((ln, ln), 0)
    col = _iota((ln, ln), 1)
    causal = col <= row
    b_cols = jnp.dot(causal.astype(F32), log_f, precision=HIGHEST, preferred_element_type=F32)
    gates_t = gates.T
    b_rows = b_cols.T
    ones = jnp.ones((ln, dh), BF16)

    for h in range(M_HEADS):
        hs = slice(h * dh, (h + 1) * dh)
        qh = act[:, hs].astype(BF16)
        kh = act[:, M_WIDTH + h * dh:M_WIDTH + (h + 1) * dh] * (dh ** -0.5)
        b_col = b_cols[:, M_HEADS + h:M_HEADS + h + 1]
        b_row = b_rows[M_HEADS + h:M_HEADS + h + 1, :]
        i_col = gates[:, h:h + 1]
        i_row = gates_t[h:h + 1, :]
        b_last = b_col[ln - 1:ln, :]
        m_st = m_sc[h][0:1, 0:1]

        d = jnp.where(causal, b_col - b_row + i_row, NEG_INF)
        g = b_col + m_st
        m_t = jnp.maximum(g, jnp.max(d, axis=1, keepdims=True))
        w_intra = jnp.exp(d - m_t)
        w_inter = jnp.exp(g - m_t)
        qk = (_nt_dot(qh, kh.astype(BF16)) * w_intra).astype(BF16)
        v_aug = jnp.concatenate([v_ref[0, :, hs], ones], axis=1)
        c_aug = c_sc[h]
        res = (w_inter * jnp.dot(qh, c_aug.astype(BF16), preferred_element_type=F32)
               + jnp.dot(qk, v_aug, preferred_element_type=F32))
        num = res[:, :dh]
        den = res[:, dh:]
        h_t = num / jnp.maximum(jnp.abs(den), jnp.exp(-m_t))

        m_new = jnp.maximum(b_last + m_st, jnp.max(b_last - b_row + i_row, axis=1, keepdims=True))
        w_k = jnp.exp(b_last - b_col + i_col - m_new)
        decay = jnp.exp(b_last + m_st - m_new)
        kw_t = (kh * w_k).T.astype(BF16)
        c_sc[h] = decay * c_aug + jnp.dot(kw_t, v_aug, preferred_element_type=F32)
        m_sc[h] = jnp.broadcast_to(m_new, (SUBLANES, LANES))

        hn = h_t * lax.rsqrt(jnp.mean(h_t * h_t, axis=1, keepdims=True) + EPS) * hw_ref[:, hs]
        y_ref[0, :, hs] = (hn * _sigmoid(o_ref[0, :, hs])).astype(BF16)


def _mlstm(mqk, mv, mo, gif, conv_w, conv_b, head_w):
    b, s, _ = mqk.shape
    ln = MLSTM_CHUNK
    assert s % ln == 0
    blk = lambda w: pl.BlockSpec((1, ln, w), lambda bi, i: (bi, i, 0))
    full = lambda a: pl.BlockSpec(a.shape, lambda bi, i: (0,) * a.ndim)
    vmem = 2 * ln * (2 * M_WIDTH * 4 + M_WIDTH * 10 + LANES * 4) + 24 * ln * ln * 4 + (8 << 20)
    return pl.pallas_call(
        _mlstm_kernel, grid=(b, s // ln),
        in_specs=[blk(2 * M_WIDTH), blk(M_WIDTH), blk(M_WIDTH), blk(LANES),
                  full(conv_w), full(conv_b), full(head_w)],
        out_specs=blk(M_WIDTH),
        out_shape=jax.ShapeDtypeStruct((b, s, M_WIDTH), BF16),
        scratch_shapes=[pltpu.VMEM((ln + SUBLANES, 2 * M_WIDTH), F32),
                        pltpu.VMEM((M_HEADS, M_HEAD_DIM, 2 * M_HEAD_DIM), F32),
                        pltpu.VMEM((M_HEADS, SUBLANES, LANES), F32)],
        compiler_params=_params(vmem, 2), name="mlstm",
    )(mqk, mv, mo, gif, conv_w, conv_b, head_w)


def _merge_kernel(ya_ref, ym_ref, gam_ref, x_ref, wb_ref, wo_ref, nw_ref, wr_ref, br_ref,
                  h_ref, xn_ref, rmeta_ref, gmeta_ref, cnt_ref, cnt_sc):
    i = pl.program_id(0)
    tm = x_ref.shape[0]

    @pl.when(i == 0)
    def _():
        cnt_sc[...] = jnp.zeros(cnt_sc.shape, F32)

    pa = jnp.dot(ya_ref[...], wb_ref[0:A_WIDTH, :], preferred_element_type=F32)
    pm = jnp.dot(ym_ref[...], wb_ref[A_WIDTH:, :], preferred_element_type=F32)
    merged = _sigmoid(gam_ref[:, 0:D_MODEL]) * pa + _sigmoid(gam_ref[:, D_MODEL:]) * pm
    h = x_ref[...] + jnp.dot(merged.astype(BF16), wo_ref[...], preferred_element_type=F32)
    h_ref[...] = h
    xn = h * lax.rsqrt(jnp.mean(h * h, axis=-1, keepdims=True) + EPS) * nw_ref[...]
    xn_ref[...] = xn

    logits = jnp.dot(xn, wr_ref[...], precision=HIGHEST, preferred_element_type=F32) + br_ref[...]
    lane = _iota((tm, LANES), 1)
    lane_f = lane.astype(F32)
    cur = logits
    vals, idxs, hits = [], [], []
    for _ in range(TOP_K):
        top = jnp.max(cur, axis=1, keepdims=True)
        first = jnp.min(jnp.where(cur == top, lane_f, float(LANES)), axis=1, keepdims=True)
        hit = lane_f == first
        vals.append(top)
        idxs.append(first)
        hits.append(hit)
        cur = jnp.where(hit, NEG_INF, cur)
    exps = [jnp.exp(v - vals[0]) for v in vals]
    inv = 1.0 / (exps[0] + exps[1] + exps[2] + exps[3])

    onehot = jnp.zeros((tm, LANES), F32)
    for hit in hits:
        onehot = jnp.where(hit, 1.0, onehot)
    before = (_iota((tm, tm), 1) < _iota((tm, tm), 0)).astype(BF16)
    seen = jnp.dot(before, onehot.astype(BF16), preferred_element_type=F32) + cnt_sc[0:1, :]
    rmeta = jnp.zeros((tm, LANES), F32)
    gmeta = jnp.zeros((tm, LANES), F32)
    for kk in range(TOP_K):
        rank = jnp.sum(jnp.where(hits[kk], seen, 0.0), axis=1, keepdims=True)
        rmeta = jnp.where(lane == kk, idxs[kk], rmeta)
        rmeta = jnp.where(lane == TOP_K + kk, rank, rmeta)
        gmeta = jnp.where(lane == kk, exps[kk] * inv, gmeta)
    rmeta_ref[...] = rmeta.astype(I32)
    gmeta_ref[...] = gmeta
    cnt_sc[...] = cnt_sc[...] + jnp.sum(onehot, axis=0, keepdims=True)
    cnt_ref[...] = cnt_sc[...]


def _merge(ya, ym, gam, x2, wb, wo, nw, wr, br):
    t = x2.shape[0]
    tm = ROW_TILE
    row = lambda w: pl.BlockSpec((tm, w), lambda i: (i, 0))
    full = lambda a: pl.BlockSpec(a.shape, lambda i: (0,) * a.ndim)
    out_shapes = (jax.ShapeDtypeStruct((t, D_MODEL), F32), jax.ShapeDtypeStruct((t, D_MODEL), F32),
                  jax.ShapeDtypeStruct((t, LANES), I32), jax.ShapeDtypeStruct((t, LANES), F32),
                  jax.ShapeDtypeStruct((SUBLANES, LANES), F32))
    vmem = 2 * (wb.size * 2 + wo.size * 2 + wr.size * 4 + tm * D_MODEL * 22) + 16 * tm * D_MODEL * 4
    return pl.pallas_call(
        _merge_kernel, grid=(t // tm,),
        in_specs=[row(A_WIDTH), row(M_WIDTH), row(2 * D_MODEL), row(D_MODEL),
                  full(wb), full(wo), full(nw), full(wr), full(br)],
        out_specs=(row(D_MODEL), row(D_MODEL), row(LANES), row(LANES),
                   pl.BlockSpec((SUBLANES, LANES), lambda i: (0, 0))),
        out_shape=out_shapes,
        scratch_shapes=[pltpu.VMEM((SUBLANES, LANES), F32)],
        compiler_params=_params(vmem), name="merge_route",
    )(ya, ym, gam, x2, wb, wo, nw, wr, br)


def _row_copy(src_ref, dst_ref, src_row, dst_row, n, sem):
    return pltpu.make_async_copy(src_ref.at[pl.ds(src_row, n)], dst_ref.at[pl.ds(dst_row, n)], sem)


def _push_kernel(tail_start_ref, tail_n_ref, n_used_ref, dest_ref, xn_ref, rows_ref, zero_sc, sem, zsem):
    i = pl.program_id(0)
    tm = xn_ref.shape[0]

    def start(t, carry):
        for kk in range(TOP_K):
            _row_copy(xn_ref, rows_ref, t, dest_ref[t * TOP_K + kk], 1, sem).start()
        return carry

    lax.fori_loop(0, tm, start, 0)

    @pl.when(i == 0)
    def _():
        zero_sc[...] = jnp.zeros(zero_sc.shape, F32)

        def fill(e, carry):
            base = tail_start_ref[e]

            def zstart(r, c):
                _row_copy(zero_sc, rows_ref, 0, base + r, 1, zsem).start()
                return c

            def zwait(r, c):
                _row_copy(zero_sc, rows_ref, 0, base + r, 1, zsem).wait()
                return c

            lax.fori_loop(0, tail_n_ref[e], zstart, 0)
            lax.fori_loop(0, tail_n_ref[e], zwait, 0)
            return carry

        lax.fori_loop(0, N_EXPERTS, fill, 0)

        def zblock(j, carry):
            cp = _row_copy(zero_sc, rows_ref, 0, pl.multiple_of(j * MOE_BLOCK, MOE_BLOCK), MOE_BLOCK, zsem)
            cp.start()
            cp.wait()
            return carry

        lax.fori_loop(n_used_ref[0], rows_ref.shape[0] // MOE_BLOCK, zblock, 0)

    def wait(t, carry):
        for kk in range(TOP_K):
            _row_copy(xn_ref, rows_ref, t, dest_ref[t * TOP_K + kk], 1, sem).wait()
        return carry

    lax.fori_loop(0, tm, wait, 0)


def _push(xn2, dest, tail_start, tail_n, n_used, n_rows):
    t = xn2.shape[0]
    tm = ROW_TILE
    grid_spec = pltpu.PrefetchScalarGridSpec(
        num_scalar_prefetch=3, grid=(t // tm,),
        in_specs=[pl.BlockSpec((tm * TOP_K,), lambda i, *_: (i,), memory_space=pltpu.SMEM),
                  pl.BlockSpec((tm, D_MODEL), lambda i, *_: (i, 0))],
        out_specs=pl.BlockSpec(memory_space=pl.ANY),
        scratch_shapes=[pltpu.VMEM((MOE_BLOCK, D_MODEL), F32),
                        pltpu.SemaphoreType.DMA(()), pltpu.SemaphoreType.DMA(())])
    return pl.pallas_call(
        _push_kernel, grid_spec=grid_spec,
        out_shape=jax.ShapeDtypeStruct((n_rows, D_MODEL), F32),
        compiler_params=_params(6 * tm * D_MODEL * 4 + (4 << 20)), name="moe_push",
    )(tail_start, tail_n, n_used, dest, xn2)


def _expert_kernel(blk_e_ref, n_used_ref, x_ref, w1_ref, b1_ref, w2_ref, b2_ref, y_ref, w1_sc, w2_sc):
    i = pl.program_id(0)
    prev = blk_e_ref[jnp.maximum(i - 1, 0)]
    fresh = jnp.logical_or(i == 0, blk_e_ref[i] != prev)

    @pl.when(jnp.logical_and(fresh, i < n_used_ref[0]))
    def _():
        w1_sc[...] = w1_ref[0].astype(BF16)
        w2_sc[...] = w2_ref[0].astype(BF16)

    @pl.when(i < n_used_ref[0])
    def _():
        xb = x_ref[...].astype(BF16)
        hdn = jnp.dot(xb, w1_sc[...], preferred_element_type=F32) + b1_ref[0]
        glu = jnp.minimum(hdn[:, :D_FF], SWIGLU_LIMIT)
        lin = jnp.clip(hdn[:, D_FF:], -SWIGLU_LIMIT, SWIGLU_LIMIT)
        act = glu * _sigmoid(SWIGLU_ALPHA * glu) * (lin + 1.0)
        y_ref[...] = jnp.dot(act.astype(BF16), w2_sc[...], preferred_element_type=F32) + b2_ref[0]

    @pl.when(i >= n_used_ref[0])
    def _():
        y_ref[...] = jnp.zeros(y_ref.shape, F32)


def _expert(x_rows, blk_expert, n_used, w1, b1, w2, b2):
    n_rows = x_rows.shape[0]
    n_blk = n_rows // MOE_BLOCK
    blk = lambda i, be, nu: (jnp.minimum(i, nu[0] - 1), 0)
    wsel = lambda i, be, nu: (be[i], 0, 0)
    grid_spec = pltpu.PrefetchScalarGridSpec(
        num_scalar_prefetch=2, grid=(n_blk,),
        in_specs=[pl.BlockSpec((MOE_BLOCK, D_MODEL), blk),
                  pl.BlockSpec((1, D_MODEL, 2 * D_FF), wsel),
                  pl.BlockSpec((1, 1, 2 * D_FF), wsel),
                  pl.BlockSpec((1, D_FF, D_MODEL), wsel),
                  pl.BlockSpec((1, 1, D_MODEL), wsel)],
        out_specs=pl.BlockSpec((MOE_BLOCK, D_MODEL), lambda i, be, nu: (i, 0)),
        scratch_shapes=[pltpu.VMEM((D_MODEL, 2 * D_FF), BF16), pltpu.VMEM((D_FF, D_MODEL), BF16)])
    vmem = 2 * (D_MODEL * 2 * D_FF * 4 + D_FF * D_MODEL * 4) + 3 * D_MODEL * D_FF * 2 + 12 * MOE_BLOCK * D_MODEL * 4
    return pl.pallas_call(
        _expert_kernel, grid_spec=grid_spec,
        out_shape=jax.ShapeDtypeStruct((n_rows, D_MODEL), F32),
        compiler_params=_params(vmem), name="moe_expert",
    )(blk_expert, n_used, x_rows, w1, b1, w2, b2)


def _final_kernel(dest_ref, h_ref, g_ref, nw_ref, rows_ref, o_ref, buf, sem):
    tm = h_ref.shape[0]

    def start(t, carry):
        for kk in range(TOP_K):
            _row_copy(rows_ref, buf.at[kk], dest_ref[t * TOP_K + kk], t, 1, sem).start()
        return carry

    lax.fori_loop(0, tm, start, 0)

    def wait(t, carry):
        for kk in range(TOP_K):
            _row_copy(rows_ref, buf.at[kk], dest_ref[t * TOP_K + kk], t, 1, sem).wait()
        return carry

    lax.fori_loop(0, tm, wait, 0)

    gates = g_ref[...]
    y = h_ref[...]
    for kk in range(TOP_K):
        y = y + gates[:, kk:kk + 1] * buf[kk]
    o_ref[...] = y * lax.rsqrt(jnp.mean(y * y, axis=-1, keepdims=True) + EPS) * nw_ref[...]


def _final(h, gmeta, nw, y_rows, dest):
    t = h.shape[0]
    tm = ROW_TILE
    return pl.pallas_call(
        _final_kernel, grid=(t // tm,),
        in_specs=[pl.BlockSpec((tm * TOP_K,), lambda i: (i,), memory_space=pltpu.SMEM),
                  pl.BlockSpec((tm, D_MODEL), lambda i: (i, 0)),
                  pl.BlockSpec((tm, LANES), lambda i: (i, 0)),
                  pl.BlockSpec(nw.shape, lambda i: (0, 0)),
                  pl.BlockSpec(memory_space=pl.ANY)],
        out_specs=pl.BlockSpec((tm, D_MODEL), lambda i: (i, 0)),
        out_shape=jax.ShapeDtypeStruct((t, D_MODEL), F32),
        scratch_shapes=[pltpu.VMEM((TOP_K, tm, D_MODEL), F32), pltpu.SemaphoreType.DMA(())],
        compiler_params=_params(10 * tm * D_MODEL * 4 + (4 << 20)), name="moe_combine",
    )(dest, h, gmeta, nw, y_rows)


def _rope_tables(seq):
    inv = np.float32(ROPE_THETA) ** (-np.arange(0, A_HEAD_DIM, 2, dtype=np.float32) / np.float32(A_HEAD_DIM))
    ang = np.arange(seq, dtype=np.float32)[:, None] * inv[None, :].astype(np.float32)
    cos = np.cos(ang).astype(np.float32)
    sin = np.sin(ang).astype(np.float32)
    return (jnp.asarray(np.concatenate([cos] * 4, axis=-1)),
            jnp.asarray(np.concatenate([-sin, sin, -sin, sin], axis=-1)))


def _route_plan(rmeta, cnt, n_tok):
    idx = rmeta[:, 0:TOP_K]
    rank = rmeta[:, TOP_K:2 * TOP_K]
    counts = cnt[0, :N_EXPERTS].astype(I32)
    padded = (counts + MOE_BLOCK - 1) // MOE_BLOCK * MOE_BLOCK
    pad_end = jnp.cumsum(padded)
    pad_start = pad_end - padded
    dest = (pad_start[idx] + rank).reshape(n_tok * TOP_K).astype(I32)
    n_blk = (n_tok * TOP_K) // MOE_BLOCK + N_EXPERTS
    blk_row = jnp.arange(n_blk, dtype=I32) * MOE_BLOCK
    blk_expert = jnp.minimum(jnp.sum((pad_end[None, :] <= blk_row[:, None]).astype(I32), axis=1),
                             N_EXPERTS - 1).astype(I32)
    n_used = (pad_end[-1:] // MOE_BLOCK).astype(I32)
    return dest, blk_expert, n_used, (pad_start + counts).astype(I32), (padded - counts).astype(I32), n_blk


def _layer(h3, norm_mix_w, w_in, m_conv_w, m_conv_b, m_gate_bias, m_head_norm_w, w_branch, w_out,
           norm_ffn_w, w_router, b_router, w_mlp1, b_mlp1, w_mlp2, b_mlp2, norm_out_w):
    b, s, d = h3.shape
    t = b * s
    x2 = h3.reshape(t, d)
    c_if = 3 * A_WIDTH + 4 * M_WIDTH
    w_main = jnp.concatenate(
        [w_in[:, :c_if], w_in[:, c_if + 2 * M_HEADS:],
         jnp.pad(w_in[:, c_if:c_if + 2 * M_HEADS], ((0, 0), (0, LANES - 2 * M_HEADS)))], axis=1).astype(BF16)
    gate_bias = jnp.pad(m_gate_bias, (0, LANES - 2 * M_HEADS)).reshape(1, LANES)
    cos_t, sin_t = _rope_tables(s)

    aq, ak, av, km, mqk, mv, mo, gam, gif = _inproj(
        x2, norm_mix_w.reshape(1, d), w_main, gate_bias, cos_t, sin_t, s)

    ya = _attn(aq.reshape(b, s, A_WIDTH), ak.reshape(b, s, A_WIDTH), av.reshape(b, s, 2 * A_WIDTH),
               km.reshape(b, s // MOBA_BLOCK, A_WIDTH))
    ym = _mlstm(mqk.reshape(b, s, 2 * M_WIDTH), mv.reshape(b, s, M_WIDTH), mo.reshape(b, s, M_WIDTH),
                gif.reshape(b, s, LANES), m_conv_w, m_conv_b.reshape(1, -1), m_head_norm_w.reshape(1, -1))

    wr = jnp.pad(w_router, ((0, 0), (0, LANES - N_EXPERTS)))
    br = jnp.concatenate([b_router, jnp.full((LANES - N_EXPERTS,), NEG, F32)]).reshape(1, LANES)
    h, xn2, rmeta, gmeta, cnt = _merge(
        ya.reshape(t, A_WIDTH), ym.reshape(t, M_WIDTH), gam, x2, w_branch.astype(BF16), w_out.astype(BF16),
        norm_ffn_w.reshape(1, d), wr, br)

    dest, blk_expert, n_used, tail_start, tail_n, n_blk = _route_plan(rmeta, cnt, t)
    x_rows = _push(xn2, dest, tail_start, tail_n, n_used, n_blk * MOE_BLOCK)
    y_rows = _expert(x_rows, blk_expert, n_used, w_mlp1, b_mlp1.reshape(N_EXPERTS, 1, -1),
                     w_mlp2, b_mlp2.reshape(N_EXPERTS, 1, -1))
    out = _final(h, gmeta, norm_out_w.reshape(1, d), y_rows, dest)
    return out.reshape(b, s, d)


def kernel(x, norm_mix_w, w_in, m_conv_w, m_conv_b, m_gate_bias, m_head_norm_w, w_branch, w_out,
           norm_ffn_w, w_router, b_router, w_mlp1, b_mlp1, w_mlp2, b_mlp2, norm_final_w):
    depth = norm_mix_w.shape[0]
    assert depth == 1, "the final RMSNorm is fused into the layer's last kernel"
    return _layer(x, norm_mix_w[0], w_in[0], m_conv_w[0], m_conv_b[0], m_gate_bias[0], m_head_norm_w[0],
                  w_branch[0], w_out[0], norm_ffn_w[0], w_router[0], b_router[0], w_mlp1[0], b_mlp1[0],
                  w_mlp2[0], b_mlp2[0], norm_final_w)
```

```python
import functools

import jax
import jax.numpy as jnp
import numpy as np
from jax import lax
from jax.experimental import pallas as pl
from jax.experimental.pallas import tpu as pltpu

F32 = jnp.float32
BF16 = jnp.bfloat16
I32 = jnp.int32
HIGHEST = lax.Precision.HIGHEST

D_MODEL = 1024
A_HEADS = 8
A_HEAD_DIM = 64
A_WIDTH = A_HEADS * A_HEAD_DIM
MOBA_BLOCK = 256
MOBA_TOPK = 3
M_HEADS = 4
M_HEAD_DIM = 128
M_WIDTH = M_HEADS * M_HEAD_DIM
M_CONV = 4
N_EXPERTS = 32
TOP_K = 4
D_FF = 1024
SWIGLU_LIMIT = 7.0
SWIGLU_ALPHA = 1.702
MOE_BLOCK = 256
ROPE_THETA = 10000.0
EPS = 1e-6
NEG = -1e30
NEG_INF = float("-inf")

LANES = 128
SUBLANES = 8
VMEM_LIMIT_CAP = 56 * 1024 * 1024

C_AQ, C_AK, C_AV = 0, 512, 1024
C_MQK, C_MV, C_MO = 1536, 2560, 3072
C_GAM, C_GIF, C_END = 3584, 5632, 5760

ROW_TILE = 256
Q_SCALE = (A_HEAD_DIM ** -0.5) * 1.4426950408889634
MLSTM_CHUNK = 256


def _params(vmem_bytes, n_axes=1):
    return pltpu.CompilerParams(
        dimension_semantics=("arbitrary",) * n_axes,
        vmem_limit_bytes=int(min(max(vmem_bytes, 16 * 1024 * 1024), VMEM_LIMIT_CAP)))


def _iota(shape, dim):
    return lax.broadcasted_iota(I32, shape, dim)


def _sigmoid(x):
    return 1.0 / (1.0 + jnp.exp(-x))


def _rows_to_tiles(ref, value):
    n = value.shape[0]
    for j in range(SUBLANES):
        ref[pl.ds(j, n, stride=SUBLANES), :] = value[:, j * LANES:(j + 1) * LANES]


def _tiles_to_rows(ref, n):
    return jnp.concatenate([ref[pl.ds(j, n, stride=SUBLANES), :] for j in range(SUBLANES)], axis=1)


def _nt_dot(a, b, precision=None):
    return lax.dot_general(a, b, (((1,), (1,)), ((), ())), precision=precision,
                           preferred_element_type=F32)


def _inproj_kernel(x_ref, nw_ref, w_ref, gb_ref, cos_ref, sin_ref,
                   aq_ref, ak_ref, av_ref, km_ref, mqk_ref, mv_ref, mo_ref, gam_ref, gif_ref):
    tm = x_ref.shape[0]
    x = x_ref[...]
    xn = x * lax.rsqrt(jnp.mean(x * x, axis=-1, keepdims=True) + EPS) * nw_ref[...]
    xb = xn.astype(BF16)

    def mm(lo, hi):
        return jnp.dot(xb, w_ref[:, lo:hi], preferred_element_type=F32)

    cos = jnp.concatenate([cos_ref[...]] * 4, axis=1)
    sin = jnp.concatenate([sin_ref[...]] * 4, axis=1)
    lane = _iota((tm, A_WIDTH), 1)
    first_half = (lane & (A_HEAD_DIM - 1)) < (A_HEAD_DIM // 2)

    def rope(t):
        up = pltpu.roll(t, A_WIDTH - A_HEAD_DIM // 2, 1)
        dn = pltpu.roll(t, A_HEAD_DIM // 2, 1)
        return t * cos + jnp.where(first_half, up, dn) * sin

    q = rope(mm(C_AQ, C_AK)) * Q_SCALE
    k = rope(mm(C_AK, C_AV))
    aq_ref[...] = q
    ak_ref[...] = k.astype(BF16)
    km_ref[0] = jnp.mean(k, axis=0, keepdims=True)

    v = mm(C_AV, C_MQK)
    lane128 = _iota((tm, LANES), 1)
    low = lane128 < A_HEAD_DIM
    for p in range(A_HEADS // 2):
        vp = v[:, p * LANES:(p + 1) * LANES]
        av_ref[:, (2 * p) * LANES:(2 * p + 1) * LANES] = jnp.where(low, vp, 1.0).astype(BF16)
        av_ref[:, (2 * p + 1) * LANES:(2 * p + 2) * LANES] = jnp.where(
            low, pltpu.roll(vp, A_HEAD_DIM, 1), 1.0).astype(BF16)

    mqk_ref[...] = mm(C_MQK, C_MV)
    mv_ref[...] = mm(C_MV, C_MO).astype(BF16)
    mo_ref[...] = mm(C_MO, C_GAM)
    gam_ref[...] = mm(C_GAM, C_GIF)
    gif_ref[...] = mm(C_GIF, C_END) + gb_ref[...]


def _inproj(x2, nw, w_main, gate_bias, cos_t, sin_t, seq):
    t = x2.shape[0]
    tm = ROW_TILE
    assert seq % tm == 0 and tm == MOBA_BLOCK
    nsteps = t // tm
    spb = seq // tm
    row = lambda w: pl.BlockSpec((tm, w), lambda i: (i, 0))
    full = lambda a: pl.BlockSpec(a.shape, lambda i: (0,) * a.ndim)
    tab = pl.BlockSpec((tm, LANES), lambda i: (i % spb, 0))
    out_shapes = (
        jax.ShapeDtypeStruct((t, A_WIDTH), F32),
        jax.ShapeDtypeStruct((t, A_WIDTH), BF16),
        jax.ShapeDtypeStruct((t, 2 * A_WIDTH), BF16),
        jax.ShapeDtypeStruct((nsteps, 1, A_WIDTH), F32),
        jax.ShapeDtypeStruct((t, 2 * M_WIDTH), F32),
        jax.ShapeDtypeStruct((t, M_WIDTH), BF16),
        jax.ShapeDtypeStruct((t, M_WIDTH), F32),
        jax.ShapeDtypeStruct((t, 2 * D_MODEL), F32),
        jax.ShapeDtypeStruct((t, LANES), F32),
    )
    out_specs = (row(A_WIDTH), row(A_WIDTH), row(2 * A_WIDTH),
                 pl.BlockSpec((1, 1, A_WIDTH), lambda i: (i, 0, 0)),
                 row(2 * M_WIDTH), row(M_WIDTH), row(M_WIDTH), row(2 * D_MODEL), row(LANES))
    vmem = 2 * (w_main.size * 2 + tm * D_MODEL * 4 + tm * C_END * 4) + 8 * tm * C_END
    return pl.pallas_call(
        _inproj_kernel, grid=(nsteps,),
        in_specs=[row(D_MODEL), full(nw), full(w_main), full(gate_bias), tab, tab],
        out_specs=out_specs, out_shape=out_shapes,
        compiler_params=_params(vmem), name="inproj",
    )(x2, nw, w_main, gate_bias, cos_t, sin_t)


def _attn_kernel(q_ref, k_ref, v_ref, km_ref, o_ref, m_sc, acc_sc, qa_sc):
    qi = pl.program_id(1)
    qc = q_ref.shape[1]
    nb = km_ref.shape[1]
    qf = q_ref[0]
    km = km_ref[0]
    kmt = jnp.concatenate([km] * A_HEADS, axis=0)
    r = _iota(kmt.shape, 0)
    c = _iota(kmt.shape, 1)
    kmt = jnp.where((r // nb) == (c // A_HEAD_DIM), kmt, 0.0)
    gate_t = _nt_dot(kmt, qf, precision=HIGHEST)

    blk = _iota((nb, qc), 0).astype(F32)
    past = _iota((nb, qc), 0) < qi
    bias_rows = []
    for h in range(A_HEADS):
        g = jnp.where(past, gate_t[h * nb:(h + 1) * nb, :], NEG_INF)
        sel = jnp.zeros((nb, qc), F32)
        for _ in range(MOBA_TOPK):
            top = jnp.max(g, axis=0, keepdims=True)
            first = jnp.min(jnp.where(g == top, blk, float(nb)), axis=0, keepdims=True)
            hit = jnp.logical_and(blk == first, top > NEG_INF)
            sel = jnp.where(hit, 1.0, sel)
            g = jnp.where(hit, NEG_INF, g)
        bias_rows.append(jnp.where(sel > 0.0, 0.0, NEG))
    if A_HEADS * nb < LANES:
        bias_rows.append(jnp.zeros((LANES - A_HEADS * nb, qc), F32))
    bias = jnp.concatenate(bias_rows, axis=0).T

    lane = _iota((qc, LANES), 1)
    klane = _iota((MOBA_BLOCK, LANES), 1)
    causal = _iota((qc, MOBA_BLOCK), 1) <= _iota((qc, MOBA_BLOCK), 0)
    own = pl.multiple_of(qi * MOBA_BLOCK, MOBA_BLOCK)

    for h in range(A_HEADS):
        ksl = slice((h // 2) * LANES, (h // 2 + 1) * LANES)
        qh = jnp.where((lane // A_HEAD_DIM) == (h % 2), qf[:, ksl], 0.0).astype(BF16)
        bh = jnp.where((lane // nb) == h, bias, 0.0).astype(BF16)
        qa_sc[h] = jnp.concatenate([qh, bh], axis=1)
        s = jnp.where(causal, _nt_dot(qh, k_ref[0, pl.ds(own, MOBA_BLOCK), ksl]), NEG)
        m0 = jnp.max(s, axis=1, keepdims=True)
        pr = jnp.exp2(s - m0)
        m_sc[h] = jnp.broadcast_to(m0, (qc, LANES))
        acc_sc[h] = jnp.dot(pr.astype(BF16), v_ref[0, pl.ds(own, MOBA_BLOCK), h * LANES:(h + 1) * LANES],
                            preferred_element_type=F32)

    def block(n):
        start = pl.multiple_of(n * MOBA_BLOCK, MOBA_BLOCK)
        onehot = jnp.where((klane % nb) == n, 1.0, 0.0).astype(BF16)
        for p in range(A_HEADS // 2):
            k_aug = jnp.concatenate([k_ref[0, pl.ds(start, MOBA_BLOCK), p * LANES:(p + 1) * LANES], onehot], axis=1)
            for h in (2 * p, 2 * p + 1):
                s = _nt_dot(qa_sc[h], k_aug)
                m_prev = m_sc[h]
                m_new = jnp.maximum(m_prev, jnp.max(s, axis=1, keepdims=True))
                alpha = jnp.exp2(m_prev - m_new)
                pr = jnp.exp2(s - jnp.concatenate([m_new, m_new], axis=1))
                pv = jnp.dot(pr.astype(BF16), v_ref[0, pl.ds(start, MOBA_BLOCK), h * LANES:(h + 1) * LANES],
                             preferred_element_type=F32)
                acc_sc[h] = alpha * acc_sc[h] + pv
                m_sc[h] = m_new

    def body(n2, carry):
        block(2 * n2)
        block(2 * n2 + 1)
        return carry

    lax.fori_loop(0, qi // 2, body, 0)

    @pl.when(qi % 2 == 1)
    def _():
        block(qi - 1)

    for p in range(A_HEADS // 2):
        a0 = acc_sc[2 * p]
        a1 = acc_sc[2 * p + 1]
        o0 = a0 / pltpu.roll(a0, A_HEAD_DIM, 1)
        o1 = a1 / pltpu.roll(a1, A_HEAD_DIM, 1)
        o_ref[0, :, p * LANES:(p + 1) * LANES] = jnp.where(
            lane < A_HEAD_DIM, o0, pltpu.roll(o1, A_HEAD_DIM, 1)).astype(BF16)


def _attn(q, k, v, km):
    b, s, _ = q.shape
    nb = s // MOBA_BLOCK
    assert nb * A_HEADS <= LANES, "block-bias columns must fit one lane group"
    qc = MOBA_BLOCK
    vmem = 2 * (s * A_WIDTH * 2 + s * 2 * A_WIDTH * 2 + qc * A_WIDTH * 6) + 16 * qc * 256 * 4 + (4 << 20)
    return pl.pallas_call(
        _attn_kernel, grid=(b, s // qc),
        in_specs=[pl.BlockSpec((1, qc, A_WIDTH), lambda bi, i: (bi, i, 0)),
                  pl.BlockSpec((1, s, A_WIDTH), lambda bi, i: (bi, 0, 0)),
                  pl.BlockSpec((1, s, 2 * A_WIDTH), lambda bi, i: (bi, 0, 0)),
                  pl.BlockSpec((1, nb, A_WIDTH), lambda bi, i: (bi, 0, 0))],
        out_specs=pl.BlockSpec((1, qc, A_WIDTH), lambda bi, i: (bi, i, 0)),
        out_shape=jax.ShapeDtypeStruct((b, s, A_WIDTH), BF16),
        scratch_shapes=[pltpu.VMEM((A_HEADS, qc, LANES), F32), pltpu.VMEM((A_HEADS, qc, LANES), F32),
                        pltpu.VMEM((A_HEADS, qc, 2 * LANES), BF16)],
        compiler_params=_params(vmem, 2), name="moba_attn",
    )(q, k, v, km)


def _mlstm_kernel(qk_ref, v_ref, o_ref, g_ref, cw_ref, cb_ref, hw_ref, y_ref, ext_sc, c_sc, m_sc):
    ci = pl.program_id(1)
    ln = qk_ref.shape[1]
    dh = M_HEAD_DIM

    @pl.when(ci == 0)
    def _():
        ext_sc[0:SUBLANES, :] = jnp.zeros((SUBLANES, 2 * M_WIDTH), F32)
        c_sc[...] = jnp.zeros(c_sc.shape, F32)
        m_sc[...] = jnp.zeros(m_sc.shape, F32)

    u = qk_ref[0]
    ext_sc[SUBLANES:SUBLANES + ln, :] = u
    conv = cb_ref[...]
    for j in range(M_CONV):
        conv = conv + cw_ref[j:j + 1, :] * ext_sc[pl.ds(SUBLANES - (M_CONV - 1) + j, ln), :]
    ext_sc[0:SUBLANES, :] = u[ln - SUBLANES:ln, :]
    act = conv * _sigmoid(conv)

    gates = g_ref[0]
    log_f = jnp.minimum(gates, 0.0) - jnp.log(1.0 + jnp.exp(-jnp.abs(gates)))
    row = _iota((ln, ln), 0)
    col = _iota((ln, ln), 1)
    causal = col <= row
    b_cols = jnp.dot(causal.astype(F32), log_f, precision=HIGHEST, preferred_element_type=F32)
    gates_t = gates.T
    b_rows = b_cols.T
    ones = jnp.ones((ln, dh), BF16)

    for h in range(M_HEADS):
        hs = slice(h * dh, (h + 1) * dh)
        qh = act[:, hs].astype(BF16)
        kh = act[:, M_WIDTH + h * dh:M_WIDTH + (h + 1) * dh] * (dh ** -0.5)
        b_col = b_cols[:, M_HEADS + h:M_HEADS + h + 1]
        b_row = b_rows[M_HEADS + h:M_HEADS + h + 1, :]
        i_col = gates[:, h:h + 1]
        i_row = gates_t[h:h + 1, :]
        b_last = b_col[ln - 1:ln, :]
        m_st = m_sc[h][0:1, 0:1]

        d = jnp.where(causal, b_col - b_row + i_row, NEG_INF)
        g = b_col + m_st
        m_t = jnp.maximum(g, jnp.max(d, axis=1, keepdims=True))
        w_intra = jnp.exp(d - m_t)
        w_inter = jnp.exp(g - m_t)
        qk = (_nt_dot(qh, kh.astype(BF16)) * w_intra).astype(BF16)
        v_aug = jnp.concatenate([v_ref[0, :, hs], ones], axis=1)
        c_aug = c_sc[h]
        res = (w_inter * jnp.dot(qh, c_aug.astype(BF16), preferred_element_type=F32)
               + jnp.dot(qk, v_aug, preferred_element_type=F32))
        num = res[:, :dh]
        den = res[:, dh:]
        h_t = num / jnp.maximum(jnp.abs(den), jnp.exp(-m_t))

        m_new = jnp.maximum(b_last + m_st, jnp.max(b_last - b_row + i_row, axis=1, keepdims=True))
        w_k = jnp.exp(b_last - b_col + i_col - m_new)
        decay = jnp.exp(b_last + m_st - m_new)
        kw_t = (kh * w_k).T.astype(BF16)
        c_sc[h] = decay * c_aug + jnp.dot(kw_t, v_aug, preferred_element_type=F32)
        m_sc[h] = jnp.broadcast_to(m_new, (SUBLANES, LANES))

        hn = h_t * lax.rsqrt(jnp.mean(h_t * h_t, axis=1, keepdims=True) + EPS) * hw_ref[:, hs]
        y_ref[0, :, hs] = (hn * _sigmoid(o_ref[0, :, hs])).astype(BF16)


def _mlstm(mqk, mv, mo, gif, conv_w, conv_b, head_w):
    b, s, _ = mqk.shape
    ln = MLSTM_CHUNK
    assert s % ln == 0
    blk = lambda w: pl.BlockSpec((1, ln, w), lambda bi, i: (bi, i, 0))
    full = lambda a: pl.BlockSpec(a.shape, lambda bi, i: (0,) * a.ndim)
    vmem = 2 * ln * (2 * M_WIDTH * 4 + M_WIDTH * 10 + LANES * 4) + 24 * ln * ln * 4 + (8 << 20)
    return pl.pallas_call(
        _mlstm_kernel, grid=(b, s // ln),
        in_specs=[blk(2 * M_WIDTH), blk(M_WIDTH), blk(M_WIDTH), blk(LANES),
                  full(conv_w), full(conv_b), full(head_w)],
        out_specs=blk(M_WIDTH),
        out_shape=jax.ShapeDtypeStruct((b, s, M_WIDTH), BF16),
        scratch_shapes=[pltpu.VMEM((ln + SUBLANES, 2 * M_WIDTH), F32),
                        pltpu.VMEM((M_HEADS, M_HEAD_DIM, 2 * M_HEAD_DIM), F32),
                        pltpu.VMEM((M_HEADS, SUBLANES, LANES), F32)],
        compiler_params=_params(vmem, 2), name="mlstm",
    )(mqk, mv, mo, gif, conv_w, conv_b, head_w)


def _merge_kernel(ya_ref, ym_ref, gam_ref, x_ref, wb_ref, wo_ref, nw_ref, wr_ref, br_ref,
                  h_ref, xn_ref, rmeta_ref, gmeta_ref, cnt_ref, cnt_sc):
    i = pl.program_id(0)
    tm = x_ref.shape[0]

    @pl.when(i == 0)
    def _():
        cnt_sc[...] = jnp.zeros(cnt_sc.shape, F32)

    pa = jnp.dot(ya_ref[...], wb_ref[0:A_WIDTH, :], preferred_element_type=F32)
    pm = jnp.dot(ym_ref[...], wb_ref[A_WIDTH:, :], preferred_element_type=F32)
    merged = _sigmoid(gam_ref[:, 0:D_MODEL]) * pa + _sigmoid(gam_ref[:, D_MODEL:]) * pm
    h = x_ref[...] + jnp.dot(merged.astype(BF16), wo_ref[...], preferred_element_type=F32)
    h_ref[...] = h
    xn = h * lax.rsqrt(jnp.mean(h * h, axis=-1, keepdims=True) + EPS) * nw_ref[...]
    _rows_to_tiles(xn_ref, xn)

    x_hi = xn.astype(BF16)
    x_lo = (xn - x_hi.astype(F32)).astype(BF16)
    both = jnp.dot(x_hi, wr_ref[...], preferred_element_type=F32)
    logits = (both[:, :LANES] + both[:, LANES:]
              + jnp.dot(x_lo, wr_ref[:, :LANES], preferred_element_type=F32) + br_ref[...])
    lane = _iota((tm, LANES), 1)
    lane_f = lane.astype(F32)
    cur = logits
    vals, idxs, hits = [], [], []
    for _ in range(TOP_K):
        top = jnp.max(cur, axis=1, keepdims=True)
        first = jnp.min(jnp.where(cur == top, lane_f, float(LANES)), axis=1, keepdims=True)
        hit = lane_f == first
        vals.append(top)
        idxs.append(first)
        hits.append(hit)
        cur = jnp.where(hit, NEG_INF, cur)
    exps = [jnp.exp(v - vals[0]) for v in vals]
    inv = 1.0 / (exps[0] + exps[1] + exps[2] + exps[3])

    onehot = jnp.zeros((tm, LANES), F32)
    for hit in hits:
        onehot = jnp.where(hit, 1.0, onehot)
    before = (_iota((tm, tm), 1) < _iota((tm, tm), 0)).astype(BF16)
    seen = jnp.dot(before, onehot.astype(BF16), preferred_element_type=F32) + cnt_sc[0:1, :]
    rmeta = jnp.zeros((tm, LANES), F32)
    gmeta = jnp.zeros((tm, LANES), F32)
    for kk in range(TOP_K):
        rank = jnp.sum(jnp.where(hits[kk], seen, 0.0), axis=1, keepdims=True)
        rmeta = jnp.where(lane == kk, idxs[kk], rmeta)
        rmeta = jnp.where(lane == TOP_K + kk, rank, rmeta)
        gmeta = jnp.where(lane == kk, exps[kk] * inv, gmeta)
    rmeta_ref[...] = rmeta.astype(I32)
    gmeta_ref[...] = gmeta
    cnt_sc[...] = cnt_sc[...] + jnp.sum(onehot, axis=0, keepdims=True)
    cnt_ref[...] = cnt_sc[...]


def _merge(ya, ym, gam, x2, wb, wo, nw, wr, br):
    t = x2.shape[0]
    tm = ROW_TILE
    row = lambda w: pl.BlockSpec((tm, w), lambda i: (i, 0))
    full = lambda a: pl.BlockSpec(a.shape, lambda i: (0,) * a.ndim)
    out_shapes = (jax.ShapeDtypeStruct((t, D_MODEL), F32), jax.ShapeDtypeStruct((t * SUBLANES, LANES), F32),
                  jax.ShapeDtypeStruct((t, LANES), I32), jax.ShapeDtypeStruct((t, LANES), F32),
                  jax.ShapeDtypeStruct((SUBLANES, LANES), F32))
    vmem = 2 * (wb.size * 2 + wo.size * 2 + wr.size * 2 + tm * D_MODEL * 22) + 16 * tm * D_MODEL * 4
    return pl.pallas_call(
        _merge_kernel, grid=(t // tm,),
        in_specs=[row(A_WIDTH), row(M_WIDTH), row(2 * D_MODEL), row(D_MODEL),
                  full(wb), full(wo), full(nw), full(wr), full(br)],
        out_specs=(row(D_MODEL), pl.BlockSpec((tm * SUBLANES, LANES), lambda i: (i, 0)), row(LANES), row(LANES),
                   pl.BlockSpec((SUBLANES, LANES), lambda i: (0, 0))),
        out_shape=out_shapes,
        scratch_shapes=[pltpu.VMEM((SUBLANES, LANES), F32)],
        compiler_params=_params(vmem), name="merge_route",
    )(ya, ym, gam, x2, wb, wo, nw, wr, br)


def _row_copy(src_ref, dst_ref, src_row, dst_row, n, sem):
    src = pl.multiple_of(src_row * SUBLANES, SUBLANES)
    dst = pl.multiple_of(dst_row * SUBLANES, SUBLANES)
    return pltpu.make_async_copy(src_ref.at[pl.ds(src, n * SUBLANES)], dst_ref.at[pl.ds(dst, n * SUBLANES)], sem)


def _push_kernel(tail_start_ref, tail_n_ref, n_used_ref, dest_ref, xn_ref, rows_ref, zero_sc, sem, zsem):
    i = pl.program_id(0)
    tm = xn_ref.shape[0] // SUBLANES

    def start(t, carry):
        for kk in range(TOP_K):
            _row_copy(xn_ref, rows_ref, t, dest_ref[t * TOP_K + kk], 1, sem).start()
        return carry

    lax.fori_loop(0, tm, start, 0)

    @pl.when(i == 0)
    def _():
        zero_sc[...] = jnp.zeros(zero_sc.shape, F32)

        def fill(e, carry):
            base = tail_start_ref[e]

            def zstart(r, c):
                _row_copy(zero_sc, rows_ref, 0, base + r, 1, zsem).start()
                return c

            def zwait(r, c):
                _row_copy(zero_sc, rows_ref, 0, base + r, 1, zsem).wait()
                return c

            lax.fori_loop(0, tail_n_ref[e], zstart, 0)
            lax.fori_loop(0, tail_n_ref[e], zwait, 0)
            return carry

        lax.fori_loop(0, N_EXPERTS, fill, 0)

        def zblock(j, carry):
            cp = _row_copy(zero_sc, rows_ref, 0, j * MOE_BLOCK, MOE_BLOCK, zsem)
            cp.start()
            cp.wait()
            return carry

        lax.fori_loop(n_used_ref[0], rows_ref.shape[0] // (MOE_BLOCK * SUBLANES), zblock, 0)

    def wait(t, carry):
        for kk in range(TOP_K):
            _row_copy(xn_ref, rows_ref, t, dest_ref[t * TOP_K + kk], 1, sem).wait()
        return carry

    lax.fori_loop(0, tm, wait, 0)


def _push(xn2, dest, tail_start, tail_n, n_used, n_rows):
    t = xn2.shape[0] // SUBLANES
    tm = ROW_TILE
    grid_spec = pltpu.PrefetchScalarGridSpec(
        num_scalar_prefetch=3, grid=(t // tm,),
        in_specs=[pl.BlockSpec((tm * TOP_K,), lambda i, *_: (i,), memory_space=pltpu.SMEM),
                  pl.BlockSpec((tm * SUBLANES, LANES), lambda i, *_: (i, 0))],
        out_specs=pl.BlockSpec(memory_space=pl.ANY),
        scratch_shapes=[pltpu.VMEM((MOE_BLOCK * SUBLANES, LANES), F32),
                        pltpu.SemaphoreType.DMA(()), pltpu.SemaphoreType.DMA(())])
    return pl.pallas_call(
        _push_kernel, grid_spec=grid_spec,
        out_shape=jax.ShapeDtypeStruct((n_rows * SUBLANES, LANES), F32),
        compiler_params=_params(6 * tm * D_MODEL * 4 + (4 << 20)), name="moe_push",
    )(tail_start, tail_n, n_used, dest, xn2)


def _expert_kernel(blk_e_ref, n_used_ref, x_ref, w1_ref, b1_ref, w2_ref, b2_ref, y_ref, w1_sc, w2_sc):
    i = pl.program_id(0)
    prev = blk_e_ref[jnp.maximum(i - 1, 0)]
    fresh = jnp.logical_or(i == 0, blk_e_ref[i] != prev)

    @pl.when(jnp.logical_and(fresh, i < n_used_ref[0]))
    def _():
        w1_sc[...] = w1_ref[0].astype(BF16)
        w2_sc[...] = w2_ref[0].astype(BF16)

    @pl.when(i < n_used_ref[0])
    def _():
        xb = _tiles_to_rows(x_ref, MOE_BLOCK).astype(BF16)
        hdn = jnp.dot(xb, w1_sc[...], preferred_element_type=F32) + b1_ref[0]
        glu = jnp.minimum(hdn[:, :D_FF], SWIGLU_LIMIT)
        lin = jnp.clip(hdn[:, D_FF:], -SWIGLU_LIMIT, SWIGLU_LIMIT)
        act = glu * _sigmoid(SWIGLU_ALPHA * glu) * (lin + 1.0)
        _rows_to_tiles(y_ref, jnp.dot(act.astype(BF16), w2_sc[...], preferred_element_type=F32) + b2_ref[0])

    @pl.when(i >= n_used_ref[0])
    def _():
        y_ref[...] = jnp.zeros(y_ref.shape, F32)


def _expert(x_rows, blk_expert, n_used, w1, b1, w2, b2):
    n_rows = x_rows.shape[0] // SUBLANES
    n_blk = n_rows // MOE_BLOCK
    blk = lambda i, be, nu: (jnp.minimum(i, nu[0] - 1), 0)
    wsel = lambda i, be, nu: (be[i], 0, 0)
    grid_spec = pltpu.PrefetchScalarGridSpec(
        num_scalar_prefetch=2, grid=(n_blk,),
        in_specs=[pl.BlockSpec((MOE_BLOCK * SUBLANES, LANES), blk),
                  pl.BlockSpec((1, D_MODEL, 2 * D_FF), wsel),
                  pl.BlockSpec((1, 1, 2 * D_FF), wsel),
                  pl.BlockSpec((1, D_FF, D_MODEL), wsel),
                  pl.BlockSpec((1, 1, D_MODEL), wsel)],
        out_specs=pl.BlockSpec((MOE_BLOCK * SUBLANES, LANES), lambda i, be, nu: (i, 0)),
        scratch_shapes=[pltpu.VMEM((D_MODEL, 2 * D_FF), BF16), pltpu.VMEM((D_FF, D_MODEL), BF16)])
    vmem = 2 * (D_MODEL * 2 * D_FF * 4 + D_FF * D_MODEL * 4) + 3 * D_MODEL * D_FF * 2 + 12 * MOE_BLOCK * D_MODEL * 4
    return pl.pallas_call(
        _expert_kernel, grid_spec=grid_spec,
        out_shape=jax.ShapeDtypeStruct((n_rows * SUBLANES, LANES), F32),
        compiler_params=_params(vmem), name="moe_expert",
    )(blk_expert, n_used, x_rows, w1, b1, w2, b2)


def _final_kernel(dest_ref, h_ref, g_ref, nw_ref, rows_ref, o_ref, buf, sem):
    tm = h_ref.shape[0]

    def start(t, carry):
        for kk in range(TOP_K):
            _row_copy(rows_ref, buf.at[kk], dest_ref[t * TOP_K + kk], t, 1, sem).start()
        return carry

    lax.fori_loop(0, tm, start, 0)

    def wait(t, carry):
        for kk in range(TOP_K):
            _row_copy(rows_ref, buf.at[kk], dest_ref[t * TOP_K + kk], t, 1, sem).wait()
        return carry

    lax.fori_loop(0, tm, wait, 0)

    gates = g_ref[...]
    y = h_ref[...]
    for kk in range(TOP_K):
        y = y + gates[:, kk:kk + 1] * _tiles_to_rows(buf.at[kk], tm)
    o_ref[...] = y * lax.rsqrt(jnp.mean(y * y, axis=-1, keepdims=True) + EPS) * nw_ref[...]


def _final(h, gmeta, nw, y_rows, dest):
    t = h.shape[0]
    tm = ROW_TILE
    return pl.pallas_call(
        _final_kernel, grid=(t // tm,),
        in_specs=[pl.BlockSpec((tm * TOP_K,), lambda i: (i,), memory_space=pltpu.SMEM),
                  pl.BlockSpec((tm, D_MODEL), lambda i: (i, 0)),
                  pl.BlockSpec((tm, LANES), lambda i: (i, 0)),
                  pl.BlockSpec(nw.shape, lambda i: (0, 0)),
                  pl.BlockSpec(memory_space=pl.ANY)],
        out_specs=pl.BlockSpec((tm, D_MODEL), lambda i: (i, 0)),
        out_shape=jax.ShapeDtypeStruct((t, D_MODEL), F32),
        scratch_shapes=[pltpu.VMEM((TOP_K, tm * SUBLANES, LANES), F32), pltpu.SemaphoreType.DMA(())],
        compiler_params=_params(10 * tm * D_MODEL * 4 + (4 << 20)), name="moe_combine",
    )(dest, h, gmeta, nw, y_rows)


def _rope_tables(seq):
    inv = np.float32(ROPE_THETA) ** (-np.arange(0, A_HEAD_DIM, 2, dtype=np.float32) / np.float32(A_HEAD_DIM))
    ang = np.arange(seq, dtype=np.float32)[:, None] * inv[None, :].astype(np.float32)
    cos = np.cos(ang).astype(np.float32)
    sin = np.sin(ang).astype(np.float32)
    return (jnp.asarray(np.concatenate([cos] * 4, axis=-1)),
            jnp.asarray(np.concatenate([-sin, sin, -sin, sin], axis=-1)))


def _route_plan(rmeta, cnt, n_tok):
    idx = rmeta[:, 0:TOP_K]
    rank = rmeta[:, TOP_K:2 * TOP_K]
    counts = cnt[0, :N_EXPERTS].astype(I32)
    padded = (counts + MOE_BLOCK - 1) // MOE_BLOCK * MOE_BLOCK
    pad_end = jnp.cumsum(padded)
    pad_start = pad_end - padded
    dest = (pad_start[idx] + rank).reshape(n_tok * TOP_K).astype(I32)
    n_blk = (n_tok * TOP_K) // MOE_BLOCK + N_EXPERTS
    blk_row = jnp.arange(n_blk, dtype=I32) * MOE_BLOCK
    blk_expert = jnp.minimum(jnp.sum((pad_end[None, :] <= blk_row[:, None]).astype(I32), axis=1),
                             N_EXPERTS - 1).astype(I32)
    n_used = (pad_end[-1:] // MOE_BLOCK).astype(I32)
    return dest, blk_expert, n_used, (pad_start + counts).astype(I32), (padded - counts).astype(I32), n_blk


def _layer(h3, norm_mix_w, w_in, m_conv_w, m_conv_b, m_gate_bias, m_head_norm_w, w_branch, w_out,
           norm_ffn_w, w_router, b_router, w_mlp1, b_mlp1, w_mlp2, b_mlp2, norm_out_w):
    b, s, d = h3.shape
    t = b * s
    x2 = h3.reshape(t, d)
    c_if = 3 * A_WIDTH + 4 * M_WIDTH
    w_main = jnp.concatenate(
        [w_in[:, :c_if], w_in[:, c_if + 2 * M_HEADS:],
         jnp.pad(w_in[:, c_if:c_if + 2 * M_HEADS], ((0, 0), (0, LANES - 2 * M_HEADS)))], axis=1).astype(BF16)
    gate_bias = jnp.pad(m_gate_bias, (0, LANES - 2 * M_HEADS)).reshape(1, LANES)
    cos_t, sin_t = _rope_tables(s)

    aq, ak, av, km, mqk, mv, mo, gam, gif = _inproj(
        x2, norm_mix_w.reshape(1, d), w_main, gate_bias, cos_t, sin_t, s)

    ya = _attn(aq.reshape(b, s, A_WIDTH), ak.reshape(b, s, A_WIDTH), av.reshape(b, s, 2 * A_WIDTH),
               km.reshape(b, s // MOBA_BLOCK, A_WIDTH))
    ym = _mlstm(mqk.reshape(b, s, 2 * M_WIDTH), mv.reshape(b, s, M_WIDTH), mo.reshape(b, s, M_WIDTH),
                gif.reshape(b, s, LANES), m_conv_w, m_conv_b.reshape(1, -1), m_head_norm_w.reshape(1, -1))

    wr = jnp.pad(w_router, ((0, 0), (0, LANES - N_EXPERTS)))
    wr_hi = wr.astype(BF16)
    wr = jnp.concatenate([wr_hi, (wr - wr_hi.astype(F32)).astype(BF16)], axis=1)
    br = jnp.concatenate([b_router, jnp.full((LANES - N_EXPERTS,), NEG, F32)]).reshape(1, LANES)
    h, xn2, rmeta, gmeta, cnt = _merge(
        ya.reshape(t, A_WIDTH), ym.reshape(t, M_WIDTH), gam, x2, w_branch.astype(BF16), w_out.astype(BF16),
        norm_ffn_w.reshape(1, d), wr, br)

    dest, blk_expert, n_used, tail_start, tail_n, n_blk = _route_plan(rmeta, cnt, t)
    x_rows = _push(xn2, dest, tail_start, tail_n, n_used, n_blk * MOE_BLOCK)
    y_rows = _expert(x_rows, blk_expert, n_used, w_mlp1, b_mlp1.reshape(N_EXPERTS, 1, -1),
                     w_mlp2, b_mlp2.reshape(N_EXPERTS, 1, -1))
    out = _final(h, gmeta, norm_out_w.reshape(1, d), y_rows, dest)
    return out.reshape(b, s, d)


def kernel(x, norm_mix_w, w_in, m_conv_w, m_conv_b, m_gate_bias, m_head_norm_w, w_branch, w_out,
           norm_ffn_w, w_router, b_router, w_mlp1, b_mlp1, w_mlp2, b_mlp2, norm_final_w):
    depth = norm_mix_w.shape[0]
    assert depth == 1, "the final RMSNorm is fused into the layer's last kernel"
    return _layer(x, norm_mix_w[0], w_in[0], m_conv_w[0], m_conv_b[0], m_gate_bias[0], m_head_norm_w[0],
                  w_branch[0], w_out[0], norm_ffn_w[0], w_router[0], b_router[0], w_mlp1[0], b_mlp1[0],
                  w_mlp2[0], b_mlp2[0], norm_final_w)
```

```python
import functools

import jax
import jax.numpy as jnp
import numpy as np
from jax import lax
from jax.experimental import pallas as pl
from jax.experimental.pallas import tpu as pltpu

F32 = jnp.float32
BF16 = jnp.bfloat16
I32 = jnp.int32
HIGHEST = lax.Precision.HIGHEST

D_MODEL = 1024
A_HEADS = 8
A_HEAD_DIM = 64
A_WIDTH = A_HEADS * A_HEAD_DIM
MOBA_BLOCK = 256
MOBA_TOPK = 3
M_HEADS = 4
M_HEAD_DIM = 128
M_WIDTH = M_HEADS * M_HEAD_DIM
M_CONV = 4
N_EXPERTS = 32
TOP_K = 4
D_FF = 1024
SWIGLU_LIMIT = 7.0
SWIGLU_ALPHA = 1.702
MOE_BLOCK = 256
ROPE_THETA = 10000.0
EPS = 1e-6
NEG = -1e30
NEG_INF = float("-inf")

LANES = 128
SUBLANES = 8
VMEM_LIMIT_CAP = 56 * 1024 * 1024

C_AQ, C_AK, C_AV = 0, 512, 1024
C_MQK, C_MV, C_MO = 1536, 2560, 3072
C_GAM, C_GIF, C_END = 3584, 5632, 5760

ROW_TILE = 256
Q_SCALE = (A_HEAD_DIM ** -0.5) * 1.4426950408889634
MLSTM_CHUNK = 256


def _params(vmem_bytes, n_axes=1):
    return pltpu.CompilerParams(
        dimension_semantics=("arbitrary",) * n_axes,
        vmem_limit_bytes=int(min(max(vmem_bytes, 16 * 1024 * 1024), VMEM_LIMIT_CAP)))


def _iota(shape, dim):
    return lax.broadcasted_iota(I32, shape, dim)


def _sigmoid(x):
    return 1.0 / (1.0 + jnp.exp(-x))


def _rows_to_tiles(ref, value):
    n = value.shape[0]
    for j in range(SUBLANES):
        ref[pl.ds(j, n, stride=SUBLANES), :] = value[:, j * LANES:(j + 1) * LANES]


def _tiles_to_rows(ref, n):
    return jnp.concatenate([ref[pl.ds(j, n, stride=SUBLANES), :] for j in range(SUBLANES)], axis=1)


def _nt_dot(a, b, precision=None):
    return lax.dot_general(a, b, (((1,), (1,)), ((), ())), precision=precision,
                           preferred_element_type=F32)


def _inproj_kernel(x_ref, nw_ref, w_ref, gb_ref, cos_ref, sin_ref,
                   aq_ref, ak_ref, av_ref, km_ref, mqk_ref, mv_ref, mo_ref, gam_ref, gif_ref):
    tm = x_ref.shape[0]
    x = x_ref[...]
    xn = x * lax.rsqrt(jnp.mean(x * x, axis=-1, keepdims=True) + EPS) * nw_ref[...]
    xb = xn.astype(BF16)

    def mm(lo, hi):
        return jnp.dot(xb, w_ref[:, lo:hi], preferred_element_type=F32)

    cos = jnp.concatenate([cos_ref[...]] * 4, axis=1)
    sin = jnp.concatenate([sin_ref[...]] * 4, axis=1)
    lane = _iota((tm, A_WIDTH), 1)
    first_half = (lane & (A_HEAD_DIM - 1)) < (A_HEAD_DIM // 2)

    def rope(t):
        up = pltpu.roll(t, A_WIDTH - A_HEAD_DIM // 2, 1)
        dn = pltpu.roll(t, A_HEAD_DIM // 2, 1)
        return t * cos + jnp.where(first_half, up, dn) * sin

    q = rope(mm(C_AQ, C_AK)) * Q_SCALE
    k = rope(mm(C_AK, C_AV))
    aq_ref[...] = q
    ak_ref[...] = k.astype(BF16)
    km_ref[0] = jnp.mean(k, axis=0, keepdims=True)

    v = mm(C_AV, C_MQK)
    lane128 = _iota((tm, LANES), 1)
    low = lane128 < A_HEAD_DIM
    for p in range(A_HEADS // 2):
        vp = v[:, p * LANES:(p + 1) * LANES]
        av_ref[:, (2 * p) * LANES:(2 * p + 1) * LANES] = jnp.where(low, vp, 1.0).astype(BF16)
        av_ref[:, (2 * p + 1) * LANES:(2 * p + 2) * LANES] = jnp.where(
            low, pltpu.roll(vp, A_HEAD_DIM, 1), 1.0).astype(BF16)

    mqk_ref[...] = mm(C_MQK, C_MV)
    mv_ref[...] = mm(C_MV, C_MO).astype(BF16)
    mo_ref[...] = mm(C_MO, C_GAM)
    gam_ref[...] = mm(C_GAM, C_GIF)
    gif_ref[...] = mm(C_GIF, C_END) + gb_ref[...]


def _inproj(x2, nw, w_main, gate_bias, cos_t, sin_t, seq):
    t = x2.shape[0]
    tm = ROW_TILE
    assert seq % tm == 0 and tm == MOBA_BLOCK
    nsteps = t // tm
    spb = seq // tm
    row = lambda w: pl.BlockSpec((tm, w), lambda i: (i, 0))
    full = lambda a: pl.BlockSpec(a.shape, lambda i: (0,) * a.ndim)
    tab = pl.BlockSpec((tm, LANES), lambda i: (i % spb, 0))
    out_shapes = (
        jax.ShapeDtypeStruct((t, A_WIDTH), F32),
        jax.ShapeDtypeStruct((t, A_WIDTH), BF16),
        jax.ShapeDtypeStruct((t, 2 * A_WIDTH), BF16),
        jax.ShapeDtypeStruct((nsteps, 1, A_WIDTH), F32),
        jax.ShapeDtypeStruct((t, 2 * M_WIDTH), F32),
        jax.ShapeDtypeStruct((t, M_WIDTH), BF16),
        jax.ShapeDtypeStruct((t, M_WIDTH), F32),
        jax.ShapeDtypeStruct((t, 2 * D_MODEL), F32),
        jax.ShapeDtypeStruct((t, LANES), F32),
    )
    out_specs = (row(A_WIDTH), row(A_WIDTH), row(2 * A_WIDTH),
                 pl.BlockSpec((1, 1, A_WIDTH), lambda i: (i, 0, 0)),
                 row(2 * M_WIDTH), row(M_WIDTH), row(M_WIDTH), row(2 * D_MODEL), row(LANES))
    vmem = 2 * (w_main.size * 2 + tm * D_MODEL * 4 + tm * C_END * 4) + 8 * tm * C_END
    return pl.pallas_call(
        _inproj_kernel, grid=(nsteps,),
        in_specs=[row(D_MODEL), full(nw), full(w_main), full(gate_bias), tab, tab],
        out_specs=out_specs, out_shape=out_shapes,
        compiler_params=_params(vmem), name="inproj",
    )(x2, nw, w_main, gate_bias, cos_t, sin_t)


def _attn_kernel(q_ref, k_ref, v_ref, km_ref, o_ref, m_sc, acc_sc, qa_sc):
    qi = pl.program_id(1)
    qc = q_ref.shape[1]
    nb = km_ref.shape[1]
    qf = q_ref[0]
    km = km_ref[0]
    kmt = jnp.concatenate([km] * A_HEADS, axis=0)
    r = _iota(kmt.shape, 0)
    c = _iota(kmt.shape, 1)
    kmt = jnp.where((r // nb) == (c // A_HEAD_DIM), kmt, 0.0)
    km_hi = kmt.astype(BF16)
    km_lo = (kmt - km_hi.astype(F32)).astype(BF16)
    q_hi = qf.astype(BF16)
    q_lo = (qf - q_hi.astype(F32)).astype(BF16)
    nrow = kmt.shape[0]
    by_hi = _nt_dot(jnp.concatenate([km_hi, km_lo], axis=0), q_hi)
    gate_t = by_hi[:nrow] + by_hi[nrow:] + _nt_dot(km_hi, q_lo)

    blk = _iota((nb, qc), 0).astype(F32)
    past = _iota((nb, qc), 0) < qi
    bias_rows = []
    for h in range(A_HEADS):
        g = jnp.where(past, gate_t[h * nb:(h + 1) * nb, :], NEG_INF)
        sel = jnp.zeros((nb, qc), F32)
        for _ in range(MOBA_TOPK):
            top = jnp.max(g, axis=0, keepdims=True)
            first = jnp.min(jnp.where(g == top, blk, float(nb)), axis=0, keepdims=True)
            hit = jnp.logical_and(blk == first, top > NEG_INF)
            sel = jnp.where(hit, 1.0, sel)
            g = jnp.where(hit, NEG_INF, g)
        bias_rows.append(jnp.where(sel > 0.0, 0.0, NEG))
    if A_HEADS * nb < LANES:
        bias_rows.append(jnp.zeros((LANES - A_HEADS * nb, qc), F32))
    bias = jnp.concatenate(bias_rows, axis=0).T

    lane = _iota((qc, LANES), 1)
    klane = _iota((MOBA_BLOCK, LANES), 1)
    causal = _iota((qc, MOBA_BLOCK), 1) <= _iota((qc, MOBA_BLOCK), 0)
    own = pl.multiple_of(qi * MOBA_BLOCK, MOBA_BLOCK)

    for h in range(A_HEADS):
        ksl = slice((h // 2) * LANES, (h // 2 + 1) * LANES)
        qh = jnp.where((lane // A_HEAD_DIM) == (h % 2), qf[:, ksl], 0.0).astype(BF16)
        bh = jnp.where((lane // nb) == h, bias, 0.0).astype(BF16)
        qa_sc[h] = jnp.concatenate([qh, bh], axis=1)
        s = jnp.where(causal, _nt_dot(qh, k_ref[0, pl.ds(own, MOBA_BLOCK), ksl]), NEG)
        m0 = jnp.max(s, axis=1, keepdims=True)
        pr = jnp.exp2(s - m0)
        m_sc[h] = jnp.broadcast_to(m0, (qc, LANES))
        acc_sc[h] = jnp.dot(pr.astype(BF16), v_ref[0, pl.ds(own, MOBA_BLOCK), h * LANES:(h + 1) * LANES],
                            preferred_element_type=F32)

    def block(n):
        start = pl.multiple_of(n * MOBA_BLOCK, MOBA_BLOCK)
        onehot = jnp.where((klane % nb) == n, 1.0, 0.0).astype(BF16)
        for p in range(A_HEADS // 2):
            k_aug = jnp.concatenate([k_ref[0, pl.ds(start, MOBA_BLOCK), p * LANES:(p + 1) * LANES], onehot], axis=1)
            for h in (2 * p, 2 * p + 1):
                s = _nt_dot(qa_sc[h], k_aug)
                m_prev = m_sc[h]
                m_new = jnp.maximum(m_prev, jnp.max(s, axis=1, keepdims=True))
                alpha = jnp.exp2(m_prev - m_new)
                pr = jnp.exp2(s - jnp.concatenate([m_new, m_new], axis=1))
                pv = jnp.dot(pr.astype(BF16), v_ref[0, pl.ds(start, MOBA_BLOCK), h * LANES:(h + 1) * LANES],
                             preferred_element_type=F32)
                acc_sc[h] = alpha * acc_sc[h] + pv
                m_sc[h] = m_new

    def body(n4, carry):
        for j in range(4):
            block(4 * n4 + j)
        return carry

    lax.fori_loop(0, qi // 4, body, 0)
    rem = qi % 4
    done = qi - rem

    @pl.when(rem >= 2)
    def _():
        block(done)
        block(done + 1)

    @pl.when(rem % 2 == 1)
    def _():
        block(qi - 1)

    for p in range(A_HEADS // 2):
        a0 = acc_sc[2 * p]
        a1 = acc_sc[2 * p + 1]
        o0 = a0 / pltpu.roll(a0, A_HEAD_DIM, 1)
        o1 = a1 / pltpu.roll(a1, A_HEAD_DIM, 1)
        o_ref[0, :, p * LANES:(p + 1) * LANES] = jnp.where(
            lane < A_HEAD_DIM, o0, pltpu.roll(o1, A_HEAD_DIM, 1)).astype(BF16)


def _attn(q, k, v, km):
    b, s, _ = q.shape
    nb = s // MOBA_BLOCK
    assert nb * A_HEADS <= LANES, "block-bias columns must fit one lane group"
    qc = MOBA_BLOCK
    vmem = 2 * (s * A_WIDTH * 2 + s * 2 * A_WIDTH * 2 + qc * A_WIDTH * 6) + 16 * qc * 256 * 4 + (4 << 20)
    return pl.pallas_call(
        _attn_kernel, grid=(b, s // qc),
        in_specs=[pl.BlockSpec((1, qc, A_WIDTH), lambda bi, i: (bi, i, 0)),
                  pl.BlockSpec((1, s, A_WIDTH), lambda bi, i: (bi, 0, 0)),
                  pl.BlockSpec((1, s, 2 * A_WIDTH), lambda bi, i: (bi, 0, 0)),
                  pl.BlockSpec((1, nb, A_WIDTH), lambda bi, i: (bi, 0, 0))],
        out_specs=pl.BlockSpec((1, qc, A_WIDTH), lambda bi, i: (bi, i, 0)),
        out_shape=jax.ShapeDtypeStruct((b, s, A_WIDTH), BF16),
        scratch_shapes=[pltpu.VMEM((A_HEADS, qc, LANES), F32), pltpu.VMEM((A_HEADS, qc, LANES), F32),
                        pltpu.VMEM((A_HEADS, qc, 2 * LANES), BF16)],
        compiler_params=_params(vmem, 2), name="moba_attn",
    )(q, k, v, km)


def _mlstm_kernel(qk_ref, v_ref, o_ref, g_ref, cw_ref, cb_ref, hw_ref, y_ref, ext_sc, c_sc, m_sc):
    ci = pl.program_id(1)
    ln = qk_ref.shape[1]
    dh = M_HEAD_DIM

    @pl.when(ci == 0)
    def _():
        ext_sc[0:SUBLANES, :] = jnp.zeros((SUBLANES, 2 * M_WIDTH), F32)
        c_sc[...] = jnp.zeros(c_sc.shape, F32)
        m_sc[...] = jnp.zeros(m_sc.shape, F32)

    u = qk_ref[0]
    ext_sc[SUBLANES:SUBLANES + ln, :] = u
    conv = cb_ref[...]
    for j in range(M_CONV):
        conv = conv + cw_ref[j:j + 1, :] * ext_sc[pl.ds(SUBLANES - (M_CONV - 1) + j, ln), :]
    ext_sc[0:SUBLANES, :] = u[ln - SUBLANES:ln, :]
    act = conv * _sigmoid(conv)

    gates = g_ref[0]
    log_f = jnp.minimum(gates, 0.0) - jnp.log(1.0 + jnp.exp(-jnp.abs(gates)))
    row = _iota((ln, ln), 0)
    col = _iota((ln, ln), 1)
    causal = col <= row
    lf_hi = log_f.astype(BF16)
    lf_r = log_f - lf_hi.astype(F32)
    lf_mid = lf_r.astype(BF16)
    lf_lo = (lf_r - lf_mid.astype(F32)).astype(BF16)
    tri = causal.astype(BF16)
    two = jnp.dot(tri, jnp.concatenate([lf_hi, lf_mid], axis=1), preferred_element_type=F32)
    b_cols = two[:, :LANES] + two[:, LANES:] + jnp.dot(tri, lf_lo, preferred_element_type=F32)
    gates_t = gates.T
    b_rows = b_cols.T
    ones = jnp.ones((ln, dh), BF16)

    for h in range(M_HEADS):
        hs = slice(h * dh, (h + 1) * dh)
        qh = act[:, hs].astype(BF16)
        kh = act[:, M_WIDTH + h * dh:M_WIDTH + (h + 1) * dh] * (dh ** -0.5)
        b_col = b_cols[:, M_HEADS + h:M_HEADS + h + 1]
        b_row = b_rows[M_HEADS + h:M_HEADS + h + 1, :]
        i_col = gates[:, h:h + 1]
        i_row = gates_t[h:h + 1, :]
        b_last = b_col[ln - 1:ln, :]
        m_st = m_sc[h][0:1, 0:1]

        d = jnp.where(causal, b_col - b_row + i_row, NEG_INF)
        g = b_col + m_st
        m_t = jnp.maximum(g, jnp.max(d, axis=1, keepdims=True))
        w_intra = jnp.exp(d - m_t)
        w_inter = jnp.exp(g - m_t)
        qk = (_nt_dot(qh, kh.astype(BF16)) * w_intra).astype(BF16)
        v_aug = jnp.concatenate([v_ref[0, :, hs], ones], axis=1)
        c_aug = c_sc[h]
        res = (w_inter * jnp.dot(qh, c_aug.astype(BF16), preferred_element_type=F32)
               + jnp.dot(qk, v_aug, preferred_element_type=F32))
        num = res[:, :dh]
        den = res[:, dh:]
        h_t = num / jnp.maximum(jnp.abs(den), jnp.exp(-m_t))

        m_new = jnp.maximum(b_last + m_st, jnp.max(b_last - b_row + i_row, axis=1, keepdims=True))
        w_k = jnp.exp(b_last - b_col + i_col - m_new)
        decay = jnp.exp(b_last + m_st - m_new)
        kw_t = (kh * w_k).T.astype(BF16)
        c_sc[h] = decay * c_aug + jnp.dot(kw_t, v_aug, preferred_element_type=F32)
        m_sc[h] = jnp.broadcast_to(m_new, (SUBLANES, LANES))

        hn = h_t * lax.rsqrt(jnp.mean(h_t * h_t, axis=1, keepdims=True) + EPS) * hw_ref[:, hs]
        y_ref[0, :, hs] = (hn * _sigmoid(o_ref[0, :, hs])).astype(BF16)


def _mlstm(mqk, mv, mo, gif, conv_w, conv_b, head_w):
    b, s, _ = mqk.shape
    ln = MLSTM_CHUNK
    assert s % ln == 0
    blk = lambda w: pl.BlockSpec((1, ln, w), lambda bi, i: (bi, i, 0))
    full = lambda a: pl.BlockSpec(a.shape, lambda bi, i: (0,) * a.ndim)
    vmem = 2 * ln * (2 * M_WIDTH * 4 + M_WIDTH * 10 + LANES * 4) + 24 * ln * ln * 4 + (8 << 20)
    return pl.pallas_call(
        _mlstm_kernel, grid=(b, s // ln),
        in_specs=[blk(2 * M_WIDTH), blk(M_WIDTH), blk(M_WIDTH), blk(LANES),
                  full(conv_w), full(conv_b), full(head_w)],
        out_specs=blk(M_WIDTH),
        out_shape=jax.ShapeDtypeStruct((b, s, M_WIDTH), BF16),
        scratch_shapes=[pltpu.VMEM((ln + SUBLANES, 2 * M_WIDTH), F32),
                        pltpu.VMEM((M_HEADS, M_HEAD_DIM, 2 * M_HEAD_DIM), F32),
                        pltpu.VMEM((M_HEADS, SUBLANES, LANES), F32)],
        compiler_params=_params(vmem, 2), name="mlstm",
    )(mqk, mv, mo, gif, conv_w, conv_b, head_w)


def _merge_kernel(ya_ref, ym_ref, gam_ref, x_ref, wb_ref, wo_ref, nw_ref, wr_ref, br_ref,
                  h_ref, xn_ref, rmeta_ref, gmeta_ref, cnt_ref, cnt_sc):
    i = pl.program_id(0)
    tm = x_ref.shape[0]

    @pl.when(i == 0)
    def _():
        cnt_sc[...] = jnp.zeros(cnt_sc.shape, F32)

    pa = jnp.dot(ya_ref[...], wb_ref[0:A_WIDTH, :], preferred_element_type=F32)
    pm = jnp.dot(ym_ref[...], wb_ref[A_WIDTH:, :], preferred_element_type=F32)
    merged = _sigmoid(gam_ref[:, 0:D_MODEL]) * pa + _sigmoid(gam_ref[:, D_MODEL:]) * pm
    h = x_ref[...] + jnp.dot(merged.astype(BF16), wo_ref[...], preferred_element_type=F32)
    h_ref[...] = h
    xn = h * lax.rsqrt(jnp.mean(h * h, axis=-1, keepdims=True) + EPS) * nw_ref[...]
    _rows_to_tiles(xn_ref, xn)

    x_hi = xn.astype(BF16)
    x_lo = (xn - x_hi.astype(F32)).astype(BF16)
    both = jnp.dot(x_hi, wr_ref[...], preferred_element_type=F32)
    logits = (both[:, :LANES] + both[:, LANES:]
              + jnp.dot(x_lo, wr_ref[:, :LANES], preferred_element_type=F32) + br_ref[...])
    lane = _iota((tm, LANES), 1)
    lane_f = lane.astype(F32)
    cur = logits
    vals, idxs, hits = [], [], []
    for _ in range(TOP_K):
        top = jnp.max(cur, axis=1, keepdims=True)
        first = jnp.min(jnp.where(cur == top, lane_f, float(LANES)), axis=1, keepdims=True)
        hit = lane_f == first
        vals.append(top)
        idxs.append(first)
        hits.append(hit)
        cur = jnp.where(hit, NEG_INF, cur)
    exps = [jnp.exp(v - vals[0]) for v in vals]
    inv = 1.0 / (exps[0] + exps[1] + exps[2] + exps[3])

    onehot = jnp.zeros((tm, LANES), F32)
    for hit in hits:
        onehot = jnp.where(hit, 1.0, onehot)
    before = (_iota((tm, tm), 1) < _iota((tm, tm), 0)).astype(BF16)
    seen = jnp.dot(before, onehot.astype(BF16), preferred_element_type=F32) + cnt_sc[0:1, :]
    rmeta = jnp.zeros((tm, LANES), F32)
    gmeta = jnp.zeros((tm, LANES), F32)
    for kk in range(TOP_K):
        rank = jnp.sum(jnp.where(hits[kk], seen, 0.0), axis=1, keepdims=True)
        rmeta = jnp.where(lane == kk, idxs[kk], rmeta)
        rmeta = jnp.where(lane == TOP_K + kk, rank, rmeta)
        gmeta = jnp.where(lane == kk, exps[kk] * inv, gmeta)
    rmeta_ref[...] = rmeta.astype(I32)
    gmeta_ref[...] = gmeta
    cnt_sc[...] = cnt_sc[...] + jnp.sum(onehot, axis=0, keepdims=True)
    cnt_ref[...] = cnt_sc[...]


def _merge(ya, ym, gam, x2, wb, wo, nw, wr, br):
    t = x2.shape[0]
    tm = ROW_TILE
    row = lambda w: pl.BlockSpec((tm, w), lambda i: (i, 0))
    full = lambda a: pl.BlockSpec(a.shape, lambda i: (0,) * a.ndim)
    out_shapes = (jax.ShapeDtypeStruct((t, D_MODEL), F32), jax.ShapeDtypeStruct((t * SUBLANES, LANES), F32),
                  jax.ShapeDtypeStruct((t, LANES), I32), jax.ShapeDtypeStruct((t, LANES), F32),
                  jax.ShapeDtypeStruct((SUBLANES, LANES), F32))
    vmem = 2 * (wb.size * 2 + wo.size * 2 + wr.size * 2 + tm * D_MODEL * 22) + 16 * tm * D_MODEL * 4
    return pl.pallas_call(
        _merge_kernel, grid=(t // tm,),
        in_specs=[row(A_WIDTH), row(M_WIDTH), row(2 * D_MODEL), row(D_MODEL),
                  full(wb), full(wo), full(nw), full(wr), full(br)],
        out_specs=(row(D_MODEL), pl.BlockSpec((tm * SUBLANES, LANES), lambda i: (i, 0)), row(LANES), row(LANES),
                   pl.BlockSpec((SUBLANES, LANES), lambda i: (0, 0))),
        out_shape=out_shapes,
        scratch_shapes=[pltpu.VMEM((SUBLANES, LANES), F32)],
        compiler_params=_params(vmem), name="merge_route",
    )(ya, ym, gam, x2, wb, wo, nw, wr, br)


def _row_copy(src_ref, dst_ref, src_row, dst_row, n, sem):
    src = pl.multiple_of(src_row * SUBLANES, SUBLANES)
    dst = pl.multiple_of(dst_row * SUBLANES, SUBLANES)
    return pltpu.make_async_copy(src_ref.at[pl.ds(src, n * SUBLANES)], dst_ref.at[pl.ds(dst, n * SUBLANES)], sem)


def _push_kernel(tail_start_ref, tail_n_ref, n_used_ref, dest_ref, xn_ref, rows_ref, zero_sc, sem, zsem):
    i = pl.program_id(0)
    tm = xn_ref.shape[0] // SUBLANES

    def start(t, carry):
        for kk in range(TOP_K):
            _row_copy(xn_ref, rows_ref, t, dest_ref[t * TOP_K + kk], 1, sem).start(priority=kk % 2)
        return carry

    lax.fori_loop(0, tm, start, 0)

    @pl.when(i == 0)
    def _():
        zero_sc[...] = jnp.zeros(zero_sc.shape, F32)

        def fill(e, carry):
            base = tail_start_ref[e]

            def zstart(r, c):
                _row_copy(zero_sc, rows_ref, 0, base + r, 1, zsem).start()
                return c

            def zwait(r, c):
                _row_copy(zero_sc, rows_ref, 0, base + r, 1, zsem).wait()
                return c

            lax.fori_loop(0, tail_n_ref[e], zstart, 0)
            lax.fori_loop(0, tail_n_ref[e], zwait, 0)
            return carry

        lax.fori_loop(0, N_EXPERTS, fill, 0)

        def zblock(j, carry):
            cp = _row_copy(zero_sc, rows_ref, 0, j * MOE_BLOCK, MOE_BLOCK, zsem)
            cp.start()
            cp.wait()
            return carry

        lax.fori_loop(n_used_ref[0], rows_ref.shape[0] // (MOE_BLOCK * SUBLANES), zblock, 0)

    def wait(t, carry):
        for kk in range(TOP_K):
            _row_copy(xn_ref, rows_ref, t, dest_ref[t * TOP_K + kk], 1, sem).wait()
        return carry

    lax.fori_loop(0, tm, wait, 0)


def _push(xn2, dest, tail_start, tail_n, n_used, n_rows):
    t = xn2.shape[0] // SUBLANES
    tm = ROW_TILE
    grid_spec = pltpu.PrefetchScalarGridSpec(
        num_scalar_prefetch=3, grid=(t // tm,),
        in_specs=[pl.BlockSpec((tm * TOP_K,), lambda i, *_: (i,), memory_space=pltpu.SMEM),
                  pl.BlockSpec((tm * SUBLANES, LANES), lambda i, *_: (i, 0))],
        out_specs=pl.BlockSpec(memory_space=pl.ANY),
        scratch_shapes=[pltpu.VMEM((MOE_BLOCK * SUBLANES, LANES), F32),
                        pltpu.SemaphoreType.DMA(()), pltpu.SemaphoreType.DMA(())])
    return pl.pallas_call(
        _push_kernel, grid_spec=grid_spec,
        out_shape=jax.ShapeDtypeStruct((n_rows * SUBLANES, LANES), F32),
        compiler_params=_params(6 * tm * D_MODEL * 4 + (4 << 20)), name="moe_push",
    )(tail_start, tail_n, n_used, dest, xn2)


def _expert_kernel(blk_e_ref, n_used_ref, x_ref, w1_ref, b1_ref, w2_ref, b2_ref, y_ref, w1_sc, w2_sc):
    i = pl.program_id(0)
    prev = blk_e_ref[jnp.maximum(i - 1, 0)]
    fresh = jnp.logical_or(i == 0, blk_e_ref[i] != prev)

    @pl.when(jnp.logical_and(fresh, i < n_used_ref[0]))
    def _():
        w1_sc[...] = w1_ref[0].astype(BF16)
        w2_sc[...] = w2_ref[0].astype(BF16)

    @pl.when(i < n_used_ref[0])
    def _():
        xb = _tiles_to_rows(x_ref, MOE_BLOCK).astype(BF16)
        hdn = jnp.dot(xb, w1_sc[...], preferred_element_type=F32) + b1_ref[0]
        glu = jnp.minimum(hdn[:, :D_FF], SWIGLU_LIMIT)
        lin = jnp.clip(hdn[:, D_FF:], -SWIGLU_LIMIT, SWIGLU_LIMIT)
        act = glu * _sigmoid(SWIGLU_ALPHA * glu) * (lin + 1.0)
        _rows_to_tiles(y_ref, jnp.dot(act.astype(BF16), w2_sc[...], preferred_element_type=F32) + b2_ref[0])

    @pl.when(i >= n_used_ref[0])
    def _():
        y_ref[...] = jnp.zeros(y_ref.shape, F32)


def _expert(x_rows, blk_expert, n_used, w1, b1, w2, b2):
    n_rows = x_rows.shape[0] // SUBLANES
    n_blk = n_rows // MOE_BLOCK
    blk = lambda i, be, nu: (jnp.minimum(i, nu[0] - 1), 0)
    wsel = lambda i, be, nu: (be[i], 0, 0)
    grid_spec = pltpu.PrefetchScalarGridSpec(
        num_scalar_prefetch=2, grid=(n_blk,),
        in_specs=[pl.BlockSpec((MOE_BLOCK * SUBLANES, LANES), blk),
                  pl.BlockSpec((1, D_MODEL, 2 * D_FF), wsel),
                  pl.BlockSpec((1, 1, 2 * D_FF), wsel),
                  pl.BlockSpec((1, D_FF, D_MODEL), wsel),
                  pl.BlockSpec((1, 1, D_MODEL), wsel)],
        out_specs=pl.BlockSpec((MOE_BLOCK * SUBLANES, LANES), lambda i, be, nu: (i, 0)),
        scratch_shapes=[pltpu.VMEM((D_MODEL, 2 * D_FF), BF16), pltpu.VMEM((D_FF, D_MODEL), BF16)])
    vmem = 2 * (D_MODEL * 2 * D_FF * 4 + D_FF * D_MODEL * 4) + 3 * D_MODEL * D_FF * 2 + 12 * MOE_BLOCK * D_MODEL * 4
    return pl.pallas_call(
        _expert_kernel, grid_spec=grid_spec,
        out_shape=jax.ShapeDtypeStruct((n_rows * SUBLANES, LANES), F32),
        compiler_params=_params(vmem), name="moe_expert",
    )(blk_expert, n_used, x_rows, w1, b1, w2, b2)


def _final_kernel(dest_ref, h_ref, g_ref, nw_ref, rows_ref, o_ref, buf, sem):
    tm = h_ref.shape[0]

    def start(t, carry):
        for kk in range(TOP_K):
            _row_copy(rows_ref, buf.at[kk], dest_ref[t * TOP_K + kk], t, 1, sem).start(priority=kk % 2)
        return carry

    lax.fori_loop(0, tm, start, 0)

    def wait(t, carry):
        for kk in range(TOP_K):
            _row_copy(rows_ref, buf.at[kk], dest_ref[t * TOP_K + kk], t, 1, sem).wait()
        return carry

    lax.fori_loop(0, tm, wait, 0)

    gates = g_ref[...]
    y = h_ref[...]
    for kk in range(TOP_K):
        y = y + gates[:, kk:kk + 1] * _tiles_to_rows(buf.at[kk], tm)
    o_ref[...] = y * lax.rsqrt(jnp.mean(y * y, axis=-1, keepdims=True) + EPS) * nw_ref[...]


def _final(h, gmeta, nw, y_rows, dest):
    t = h.shape[0]
    tm = ROW_TILE
    return pl.pallas_call(
        _final_kernel, grid=(t // tm,),
        in_specs=[pl.BlockSpec((tm * TOP_K,), lambda i: (i,), memory_space=pltpu.SMEM),
                  pl.BlockSpec((tm, D_MODEL), lambda i: (i, 0)),
                  pl.BlockSpec((tm, LANES), lambda i: (i, 0)),
                  pl.BlockSpec(nw.shape, lambda i: (0, 0)),
                  pl.BlockSpec(memory_space=pl.ANY)],
        out_specs=pl.BlockSpec((tm, D_MODEL), lambda i: (i, 0)),
        out_shape=jax.ShapeDtypeStruct((t, D_MODEL), F32),
        scratch_shapes=[pltpu.VMEM((TOP_K, tm * SUBLANES, LANES), F32), pltpu.SemaphoreType.DMA(())],
        compiler_params=_params(10 * tm * D_MODEL * 4 + (4 << 20)), name="moe_combine",
    )(dest, h, gmeta, nw, y_rows)


def _rope_tables(seq):
    inv = np.float32(ROPE_THETA) ** (-np.arange(0, A_HEAD_DIM, 2, dtype=np.float32) / np.float32(A_HEAD_DIM))
    ang = np.arange(seq, dtype=np.float32)[:, None] * inv[None, :].astype(np.float32)
    cos = np.cos(ang).astype(np.float32)
    sin = np.sin(ang).astype(np.float32)
    return (jnp.asarray(np.concatenate([cos] * 4, axis=-1)),
            jnp.asarray(np.concatenate([-sin, sin, -sin, sin], axis=-1)))


def _route_plan(rmeta, cnt, n_tok):
    idx = rmeta[:, 0:TOP_K]
    rank = rmeta[:, TOP_K:2 * TOP_K]
    counts = cnt[0, :N_EXPERTS].astype(I32)
    padded = (counts + MOE_BLOCK - 1) // MOE_BLOCK * MOE_BLOCK
    pad_end = jnp.cumsum(padded)
    pad_start = pad_end - padded
    dest = (pad_start[idx] + rank).reshape(n_tok * TOP_K).astype(I32)
    n_blk = (n_tok * TOP_K) // MOE_BLOCK + N_EXPERTS
    blk_row = jnp.arange(n_blk, dtype=I32) * MOE_BLOCK
    blk_expert = jnp.minimum(jnp.sum((pad_end[None, :] <= blk_row[:, None]).astype(I32), axis=1),
                             N_EXPERTS - 1).astype(I32)
    n_used = (pad_end[-1:] // MOE_BLOCK).astype(I32)
    return dest, blk_expert, n_used, (pad_start + counts).astype(I32), (padded - counts).astype(I32), n_blk


def _layer(h3, norm_mix_w, w_in, m_conv_w, m_conv_b, m_gate_bias, m_head_norm_w, w_branch, w_out,
           norm_ffn_w, w_router, b_router, w_mlp1, b_mlp1, w_mlp2, b_mlp2, norm_out_w):
    b, s, d = h3.shape
    t = b * s
    x2 = h3.reshape(t, d)
    c_if = 3 * A_WIDTH + 4 * M_WIDTH
    w_main = jnp.concatenate(
        [w_in[:, :c_if], w_in[:, c_if + 2 * M_HEADS:],
         jnp.pad(w_in[:, c_if:c_if + 2 * M_HEADS], ((0, 0), (0, LANES - 2 * M_HEADS)))], axis=1).astype(BF16)
    gate_bias = jnp.pad(m_gate_bias, (0, LANES - 2 * M_HEADS)).reshape(1, LANES)
    cos_t, sin_t = _rope_tables(s)

    aq, ak, av, km, mqk, mv, mo, gam, gif = _inproj(
        x2, norm_mix_w.reshape(1, d), w_main, gate_bias, cos_t, sin_t, s)

    ya = _attn(aq.reshape(b, s, A_WIDTH), ak.reshape(b, s, A_WIDTH), av.reshape(b, s, 2 * A_WIDTH),
               km.reshape(b, s // MOBA_BLOCK, A_WIDTH))
    ym = _mlstm(mqk.reshape(b, s, 2 * M_WIDTH), mv.reshape(b, s, M_WIDTH), mo.reshape(b, s, M_WIDTH),
                gif.reshape(b, s, LANES), m_conv_w, m_conv_b.reshape(1, -1), m_head_norm_w.reshape(1, -1))

    wr = jnp.pad(w_router, ((0, 0), (0, LANES - N_EXPERTS)))
    wr_hi = wr.astype(BF16)
    wr = jnp.concatenate([wr_hi, (wr - wr_hi.astype(F32)).astype(BF16)], axis=1)
    br = jnp.concatenate([b_router, jnp.full((LANES - N_EXPERTS,), NEG, F32)]).reshape(1, LANES)
    h, xn2, rmeta, gmeta, cnt = _merge(
        ya.reshape(t, A_WIDTH), ym.reshape(t, M_WIDTH), gam, x2, w_branch.astype(BF16), w_out.astype(BF16),
        norm_ffn_w.reshape(1, d), wr, br)

    dest, blk_expert, n_used, tail_start, tail_n, n_blk = _route_plan(rmeta, cnt, t)
    x_rows = _push(xn2, dest, tail_start, tail_n, n_used, n_blk * MOE_BLOCK)
    y_rows = _expert(x_rows, blk_expert, n_used, w_mlp1, b_mlp1.reshape(N_EXPERTS, 1, -1),
                     w_mlp2, b_mlp2.reshape(N_EXPERTS, 1, -1))
    out = _final(h, gmeta, norm_out_w.reshape(1, d), y_rows, dest)
    return out.reshape(b, s, d)


def kernel(x, norm_mix_w, w_in, m_conv_w, m_conv_b, m_gate_bias, m_head_norm_w, w_branch, w_out,
           norm_ffn_w, w_router, b_router, w_mlp1, b_mlp1, w_mlp2, b_mlp2, norm_final_w):
    depth = norm_mix_w.shape[0]
    assert depth == 1, "the final RMSNorm is fused into the layer's last kernel"
    return _layer(x, norm_mix_w[0], w_in[0], m_conv_w[0], m_conv_b[0], m_gate_bias[0], m_head_norm_w[0],
                  w_branch[0], w_out[0], norm_ffn_w[0], w_router[0], b_router[0], w_mlp1[0], b_mlp1[0],
                  w_mlp2[0], b_mlp2[0], norm_final_w)
```

```python
import functools

import jax
import jax.numpy as jnp
import numpy as np
from jax import lax
from jax.experimental import pallas as pl
from jax.experimental.pallas import tpu as pltpu

F32 = jnp.float32
BF16 = jnp.bfloat16
I32 = jnp.int32
HIGHEST = lax.Precision.HIGHEST

D_MODEL = 1024
A_HEADS = 8
A_HEAD_DIM = 64
A_WIDTH = A_HEADS * A_HEAD_DIM
MOBA_BLOCK = 256
MOBA_TOPK = 3
M_HEADS = 4
M_HEAD_DIM = 128
M_WIDTH = M_HEADS * M_HEAD_DIM
M_CONV = 4
N_EXPERTS = 32
TOP_K = 4
D_FF = 1024
SWIGLU_LIMIT = 7.0
SWIGLU_ALPHA = 1.702
MOE_BLOCK = 256
ROPE_THETA = 10000.0
EPS = 1e-6
NEG = -1e30
NEG_INF = float("-inf")

LANES = 128
SUBLANES = 8
VMEM_LIMIT_CAP = 56 * 1024 * 1024

C_AQ, C_AK, C_AV = 0, 512, 1024
C_MQK, C_MV, C_MO = 1536, 2560, 3072
C_GAM, C_GIF, C_END = 3584, 5632, 5760

ROW_TILE = 256
Q_SCALE = (A_HEAD_DIM ** -0.5) * 1.4426950408889634
MLSTM_CHUNK = 256


def _params(vmem_bytes, n_axes=1):
    return pltpu.CompilerParams(
        dimension_semantics=("arbitrary",) * n_axes,
        vmem_limit_bytes=int(min(max(vmem_bytes, 16 * 1024 * 1024), VMEM_LIMIT_CAP)))


def _iota(shape, dim):
    return lax.broadcasted_iota(I32, shape, dim)


def _sigmoid(x):
    return 1.0 / (1.0 + jnp.exp(-x))


def _rows_to_tiles(ref, value):
    n = value.shape[0]
    for j in range(SUBLANES):
        ref[pl.ds(j, n, stride=SUBLANES), :] = value[:, j * LANES:(j + 1) * LANES]


def _tiles_to_rows(ref, n):
    return jnp.concatenate([ref[pl.ds(j, n, stride=SUBLANES), :] for j in range(SUBLANES)], axis=1)


def _nt_dot(a, b, precision=None):
    return lax.dot_general(a, b, (((1,), (1,)), ((), ())), precision=precision,
                           preferred_element_type=F32)


def _inproj_kernel(x_ref, nw_ref, w_ref, gb_ref, cos_ref, sin_ref,
                   aq_ref, ak_ref, av_ref, km_ref, mqk_ref, mv_ref, mo_ref, gam_ref, gif_ref):
    tm = x_ref.shape[0]
    x = x_ref[...]
    xn = x * lax.rsqrt(jnp.mean(x * x, axis=-1, keepdims=True) + EPS) * nw_ref[...]
    xb = xn.astype(BF16)

    def mm(lo, hi):
        return jnp.dot(xb, w_ref[:, lo:hi], preferred_element_type=F32)

    cos = jnp.concatenate([cos_ref[...]] * 4, axis=1)
    sin = jnp.concatenate([sin_ref[...]] * 4, axis=1)
    lane = _iota((tm, A_WIDTH), 1)
    first_half = (lane & (A_HEAD_DIM - 1)) < (A_HEAD_DIM // 2)

    def rope(t):
        up = pltpu.roll(t, A_WIDTH - A_HEAD_DIM // 2, 1)
        dn = pltpu.roll(t, A_HEAD_DIM // 2, 1)
        return t * cos + jnp.where(first_half, up, dn) * sin

    q = rope(mm(C_AQ, C_AK)) * Q_SCALE
    k = rope(mm(C_AK, C_AV))
    aq_ref[...] = q
    ak_ref[...] = k.astype(BF16)
    km_ref[0] = jnp.mean(k, axis=0, keepdims=True)

    v = mm(C_AV, C_MQK)
    lane128 = _iota((tm, LANES), 1)
    low = lane128 < A_HEAD_DIM
    for p in range(A_HEADS // 2):
        vp = v[:, p * LANES:(p + 1) * LANES]
        av_ref[:, (2 * p) * LANES:(2 * p + 1) * LANES] = jnp.where(low, vp, 1.0).astype(BF16)
        av_ref[:, (2 * p + 1) * LANES:(2 * p + 2) * LANES] = jnp.where(
            low, pltpu.roll(vp, A_HEAD_DIM, 1), 1.0).astype(BF16)

    mqk_ref[...] = mm(C_MQK, C_MV)
    mv_ref[...] = mm(C_MV, C_MO).astype(BF16)
    mo_ref[...] = mm(C_MO, C_GAM)
    gam_ref[...] = mm(C_GAM, C_GIF)
    gif_ref[...] = mm(C_GIF, C_END) + gb_ref[...]


def _inproj(x2, nw, w_main, gate_bias, cos_t, sin_t, seq):
    t = x2.shape[0]
    tm = ROW_TILE
    assert seq % tm == 0 and tm == MOBA_BLOCK
    nsteps = t // tm
    spb = seq // tm
    row = lambda w: pl.BlockSpec((tm, w), lambda i: (i, 0))
    full = lambda a: pl.BlockSpec(a.shape, lambda i: (0,) * a.ndim)
    tab = pl.BlockSpec((tm, LANES), lambda i: (i % spb, 0))
    out_shapes = (
        jax.ShapeDtypeStruct((t, A_WIDTH), F32),
        jax.ShapeDtypeStruct((t, A_WIDTH), BF16),
        jax.ShapeDtypeStruct((t, 2 * A_WIDTH), BF16),
        jax.ShapeDtypeStruct((nsteps, 1, A_WIDTH), F32),
        jax.ShapeDtypeStruct((t, 2 * M_WIDTH), F32),
        jax.ShapeDtypeStruct((t, M_WIDTH), BF16),
        jax.ShapeDtypeStruct((t, M_WIDTH), F32),
        jax.ShapeDtypeStruct((t, 2 * D_MODEL), F32),
        jax.ShapeDtypeStruct((t, LANES), F32),
    )
    out_specs = (row(A_WIDTH), row(A_WIDTH), row(2 * A_WIDTH),
                 pl.BlockSpec((1, 1, A_WIDTH), lambda i: (i, 0, 0)),
                 row(2 * M_WIDTH), row(M_WIDTH), row(M_WIDTH), row(2 * D_MODEL), row(LANES))
    vmem = 2 * (w_main.size * 2 + tm * D_MODEL * 4 + tm * C_END * 4) + 8 * tm * C_END
    return pl.pallas_call(
        _inproj_kernel, grid=(nsteps,),
        in_specs=[row(D_MODEL), full(nw), full(w_main), full(gate_bias), tab, tab],
        out_specs=out_specs, out_shape=out_shapes,
        compiler_params=_params(vmem), name="inproj",
    )(x2, nw, w_main, gate_bias, cos_t, sin_t)


def _attn_kernel(q_ref, k_ref, v_ref, km_ref, o_ref, m_sc, acc_sc, qa_sc):
    qi = pl.program_id(1)
    qc = q_ref.shape[1]
    nb = km_ref.shape[1]
    qf = q_ref[0]
    km = km_ref[0]
    kmt = jnp.concatenate([km] * A_HEADS, axis=0)
    r = _iota(kmt.shape, 0)
    c = _iota(kmt.shape, 1)
    kmt = jnp.where((r // nb) == (c // A_HEAD_DIM), kmt, 0.0)
    km_hi = kmt.astype(BF16)
    km_lo = (kmt - km_hi.astype(F32)).astype(BF16)
    q_hi = qf.astype(BF16)
    q_lo = (qf - q_hi.astype(F32)).astype(BF16)
    nrow = kmt.shape[0]
    by_hi = _nt_dot(jnp.concatenate([km_hi, km_lo], axis=0), q_hi)
    gate_t = by_hi[:nrow] + by_hi[nrow:] + _nt_dot(km_hi, q_lo)

    blk = _iota((nb, qc), 0).astype(F32)
    past = _iota((nb, qc), 0) < qi
    bias_rows = []
    for h in range(A_HEADS):
        g = jnp.where(past, gate_t[h * nb:(h + 1) * nb, :], NEG_INF)
        sel = jnp.zeros((nb, qc), F32)
        for _ in range(MOBA_TOPK):
            top = jnp.max(g, axis=0, keepdims=True)
            first = jnp.min(jnp.where(g == top, blk, float(nb)), axis=0, keepdims=True)
            hit = jnp.logical_and(blk == first, top > NEG_INF)
            sel = jnp.where(hit, 1.0, sel)
            g = jnp.where(hit, NEG_INF, g)
        bias_rows.append(jnp.where(sel > 0.0, 0.0, NEG))
    if A_HEADS * nb < LANES:
        bias_rows.append(jnp.zeros((LANES - A_HEADS * nb, qc), F32))
    bias = jnp.concatenate(bias_rows, axis=0).T

    lane = _iota((qc, LANES), 1)
    klane = _iota((MOBA_BLOCK, LANES), 1)
    causal = _iota((qc, MOBA_BLOCK), 1) <= _iota((qc, MOBA_BLOCK), 0)
    own = pl.multiple_of(qi * MOBA_BLOCK, MOBA_BLOCK)

    for h in range(A_HEADS):
        ksl = slice((h // 2) * LANES, (h // 2 + 1) * LANES)
        qh = jnp.where((lane // A_HEAD_DIM) == (h % 2), qf[:, ksl], 0.0).astype(BF16)
        bh = jnp.where((lane // nb) == h, bias, 0.0).astype(BF16)
        qa_sc[h] = jnp.concatenate([qh, bh], axis=1)
        s = jnp.where(causal, _nt_dot(qh, k_ref[0, pl.ds(own, MOBA_BLOCK), ksl]), NEG)
        m0 = jnp.max(s, axis=1, keepdims=True)
        pr = jnp.exp2(s - m0)
        m_sc[h] = jnp.broadcast_to(m0, (qc, LANES))
        acc_sc[h] = jnp.dot(pr.astype(BF16), v_ref[0, pl.ds(own, MOBA_BLOCK), h * LANES:(h + 1) * LANES],
                            preferred_element_type=F32)

    def block(n):
        start = pl.multiple_of(n * MOBA_BLOCK, MOBA_BLOCK)
        onehot = jnp.where((klane % nb) == n, 1.0, 0.0).astype(BF16)
        for p in range(A_HEADS // 2):
            k_aug = jnp.concatenate([k_ref[0, pl.ds(start, MOBA_BLOCK), p * LANES:(p + 1) * LANES], onehot], axis=1)
            for h in (2 * p, 2 * p + 1):
                s = _nt_dot(qa_sc[h], k_aug)
                m_prev = m_sc[h]
                m_new = jnp.maximum(m_prev, jnp.max(s, axis=1, keepdims=True))
                alpha = jnp.exp2(m_prev - m_new)
                pr = jnp.exp2(s - jnp.concatenate([m_new, m_new], axis=1))
                pv = jnp.dot(pr.astype(BF16), v_ref[0, pl.ds(start, MOBA_BLOCK), h * LANES:(h + 1) * LANES],
                             preferred_element_type=F32)
                acc_sc[h] = alpha * acc_sc[h] + pv
                m_sc[h] = m_new

    def body(n4, carry):
        for j in range(4):
            block(4 * n4 + j)
        return carry

    lax.fori_loop(0, qi // 4, body, 0)
    rem = qi % 4
    done = qi - rem

    @pl.when(rem >= 2)
    def _():
        block(done)
        block(done + 1)

    @pl.when(rem % 2 == 1)
    def _():
        block(qi - 1)

    for p in range(A_HEADS // 2):
        a0 = acc_sc[2 * p]
        a1 = acc_sc[2 * p + 1]
        o0 = a0 / pltpu.roll(a0, A_HEAD_DIM, 1)
        o1 = a1 / pltpu.roll(a1, A_HEAD_DIM, 1)
        o_ref[0, :, p * LANES:(p + 1) * LANES] = jnp.where(
            lane < A_HEAD_DIM, o0, pltpu.roll(o1, A_HEAD_DIM, 1)).astype(BF16)


def _attn(q, k, v, km):
    b, s, _ = q.shape
    nb = s // MOBA_BLOCK
    assert nb * A_HEADS <= LANES, "block-bias columns must fit one lane group"
    qc = MOBA_BLOCK
    vmem = 2 * (s * A_WIDTH * 2 + s * 2 * A_WIDTH * 2 + qc * A_WIDTH * 6) + 16 * qc * 256 * 4 + (4 << 20)
    return pl.pallas_call(
        _attn_kernel, grid=(b, s // qc),
        in_specs=[pl.BlockSpec((1, qc, A_WIDTH), lambda bi, i: (bi, i, 0)),
                  pl.BlockSpec((1, s, A_WIDTH), lambda bi, i: (bi, 0, 0)),
                  pl.BlockSpec((1, s, 2 * A_WIDTH), lambda bi, i: (bi, 0, 0)),
                  pl.BlockSpec((1, nb, A_WIDTH), lambda bi, i: (bi, 0, 0))],
        out_specs=pl.BlockSpec((1, qc, A_WIDTH), lambda bi, i: (bi, i, 0)),
        out_shape=jax.ShapeDtypeStruct((b, s, A_WIDTH), BF16),
        scratch_shapes=[pltpu.VMEM((A_HEADS, qc, LANES), F32), pltpu.VMEM((A_HEADS, qc, LANES), F32),
                        pltpu.VMEM((A_HEADS, qc, 2 * LANES), BF16)],
        compiler_params=_params(vmem, 2), name="moba_attn",
    )(q, k, v, km)


def _mlstm_kernel(qk_ref, v_ref, o_ref, g_ref, cw_ref, cb_ref, hw_ref, y_ref, ext_sc, c_sc, m_sc):
    ci = pl.program_id(1)
    ln = qk_ref.shape[1]
    dh = M_HEAD_DIM

    @pl.when(ci == 0)
    def _():
        ext_sc[0:SUBLANES, :] = jnp.zeros((SUBLANES, 2 * M_WIDTH), F32)
        c_sc[...] = jnp.zeros(c_sc.shape, F32)
        m_sc[...] = jnp.zeros(m_sc.shape, F32)

    u = qk_ref[0]
    ext_sc[SUBLANES:SUBLANES + ln, :] = u
    conv = cb_ref[...]
    for j in range(M_CONV):
        conv = conv + cw_ref[j:j + 1, :] * ext_sc[pl.ds(SUBLANES - (M_CONV - 1) + j, ln), :]
    ext_sc[0:SUBLANES, :] = u[ln - SUBLANES:ln, :]
    act = conv * _sigmoid(conv)

    gates = g_ref[0]
    log_f = jnp.minimum(gates, 0.0) - jnp.log(1.0 + jnp.exp(-jnp.abs(gates)))
    row = _iota((ln, ln), 0)
    col = _iota((ln, ln), 1)
    causal = col <= row
    lf_hi = log_f.astype(BF16)
    lf_r = log_f - lf_hi.astype(F32)
    lf_mid = lf_r.astype(BF16)
    lf_lo = (lf_r - lf_mid.astype(F32)).astype(BF16)
    tri = causal.astype(BF16)
    two = jnp.dot(tri, jnp.concatenate([lf_hi, lf_mid], axis=1), preferred_element_type=F32)
    b_cols = two[:, :LANES] + two[:, LANES:] + jnp.dot(tri, lf_lo, preferred_element_type=F32)
    gates_t = gates.T
    b_rows = b_cols.T
    ones = jnp.ones((ln, dh), BF16)

    for h in range(M_HEADS):
        hs = slice(h * dh, (h + 1) * dh)
        qh = act[:, hs].astype(BF16)
        kh = act[:, M_WIDTH + h * dh:M_WIDTH + (h + 1) * dh] * (dh ** -0.5)
        b_col = b_cols[:, M_HEADS + h:M_HEADS + h + 1]
        b_row = b_rows[M_HEADS + h:M_HEADS + h + 1, :]
        i_col = gates[:, h:h + 1]
        i_row = gates_t[h:h + 1, :]
        b_last = b_col[ln - 1:ln, :]
        m_st = m_sc[h][0:1, 0:1]

        d = jnp.where(causal, b_col - b_row + i_row, NEG_INF)
        g = b_col + m_st
        m_t = jnp.maximum(g, jnp.max(d, axis=1, keepdims=True))
        w_intra = jnp.exp(d - m_t)
        w_inter = jnp.exp(g - m_t)
        qk = (_nt_dot(qh, kh.astype(BF16)) * w_intra).astype(BF16)
        v_aug = jnp.concatenate([v_ref[0, :, hs], ones], axis=1)
        c_aug = c_sc[h]
        res = (w_inter * jnp.dot(qh, c_aug.astype(BF16), preferred_element_type=F32)
               + jnp.dot(qk, v_aug, preferred_element_type=F32))
        num = res[:, :dh]
        den = res[:, dh:]
        h_t = num / jnp.maximum(jnp.abs(den), jnp.exp(-m_t))

        m_new = jnp.maximum(b_last + m_st, jnp.max(b_last - b_row + i_row, axis=1, keepdims=True))
        w_k = jnp.exp(b_last - b_col + i_col - m_new)
        decay = jnp.exp(b_last + m_st - m_new)
        kw_t = (kh * w_k).T.astype(BF16)
        c_sc[h] = decay * c_aug + jnp.dot(kw_t, v_aug, preferred_element_type=F32)
        m_sc[h] = jnp.broadcast_to(m_new, (SUBLANES, LANES))

        hn = h_t * lax.rsqrt(jnp.mean(h_t * h_t, axis=1, keepdims=True) + EPS) * hw_ref[:, hs]
        y_ref[0, :, hs] = (hn * _sigmoid(o_ref[0, :, hs])).astype(BF16)


def _mlstm(mqk, mv, mo, gif, conv_w, conv_b, head_w):
    b, s, _ = mqk.shape
    ln = MLSTM_CHUNK
    assert s % ln == 0
    blk = lambda w: pl.BlockSpec((1, ln, w), lambda bi, i: (bi, i, 0))
    full = lambda a: pl.BlockSpec(a.shape, lambda bi, i: (0,) * a.ndim)
    vmem = 2 * ln * (2 * M_WIDTH * 4 + M_WIDTH * 10 + LANES * 4) + 24 * ln * ln * 4 + (8 << 20)
    return pl.pallas_call(
        _mlstm_kernel, grid=(b, s // ln),
        in_specs=[blk(2 * M_WIDTH), blk(M_WIDTH), blk(M_WIDTH), blk(LANES),
                  full(conv_w), full(conv_b), full(head_w)],
        out_specs=blk(M_WIDTH),
        out_shape=jax.ShapeDtypeStruct((b, s, M_WIDTH), BF16),
        scratch_shapes=[pltpu.VMEM((ln + SUBLANES, 2 * M_WIDTH), F32),
                        pltpu.VMEM((M_HEADS, M_HEAD_DIM, 2 * M_HEAD_DIM), F32),
                        pltpu.VMEM((M_HEADS, SUBLANES, LANES), F32)],
        compiler_params=_params(vmem, 2), name="mlstm",
    )(mqk, mv, mo, gif, conv_w, conv_b, head_w)


def _merge_kernel(ya_ref, ym_ref, gam_ref, x_ref, wb_ref, wo_ref, nw_ref, wr_ref, br_ref,
                  h_ref, xn_ref, rmeta_ref, gmeta_ref, cnt_ref, cnt_sc):
    i = pl.program_id(0)
    tm = x_ref.shape[0]

    @pl.when(i == 0)
    def _():
        cnt_sc[...] = jnp.zeros(cnt_sc.shape, F32)

    pa = jnp.dot(ya_ref[...], wb_ref[0:A_WIDTH, :], preferred_element_type=F32)
    pm = jnp.dot(ym_ref[...], wb_ref[A_WIDTH:, :], preferred_element_type=F32)
    merged = _sigmoid(gam_ref[:, 0:D_MODEL]) * pa + _sigmoid(gam_ref[:, D_MODEL:]) * pm
    h = x_ref[...] + jnp.dot(merged.astype(BF16), wo_ref[...], preferred_element_type=F32)
    h_ref[...] = h
    xn = h * lax.rsqrt(jnp.mean(h * h, axis=-1, keepdims=True) + EPS) * nw_ref[...]
    _rows_to_tiles(xn_ref, xn)

    x_hi = xn.astype(BF16)
    x_lo = (xn - x_hi.astype(F32)).astype(BF16)
    both = jnp.dot(x_hi, wr_ref[...], preferred_element_type=F32)
    logits = (both[:, :LANES] + both[:, LANES:]
              + jnp.dot(x_lo, wr_ref[:, :LANES], preferred_element_type=F32) + br_ref[...])
    lane = _iota((tm, LANES), 1)
    lane_f = lane.astype(F32)
    cur = logits
    vals, idxs, hits = [], [], []
    for _ in range(TOP_K):
        top = jnp.max(cur, axis=1, keepdims=True)
        first = jnp.min(jnp.where(cur == top, lane_f, float(LANES)), axis=1, keepdims=True)
        hit = lane_f == first
        vals.append(top)
        idxs.append(first)
        hits.append(hit)
        cur = jnp.where(hit, NEG_INF, cur)
    exps = [jnp.exp(v - vals[0]) for v in vals]
    inv = 1.0 / (exps[0] + exps[1] + exps[2] + exps[3])

    onehot = jnp.zeros((tm, LANES), F32)
    for hit in hits:
        onehot = jnp.where(hit, 1.0, onehot)
    before = (_iota((tm, tm), 1) < _iota((tm, tm), 0)).astype(BF16)
    seen = jnp.dot(before, onehot.astype(BF16), preferred_element_type=F32) + cnt_sc[0:1, :]
    rmeta = jnp.zeros((tm, LANES), F32)
    gmeta = jnp.zeros((tm, LANES), F32)
    for kk in range(TOP_K):
        rank = jnp.sum(jnp.where(hits[kk], seen, 0.0), axis=1, keepdims=True)
        rmeta = jnp.where(lane == kk, idxs[kk], rmeta)
        rmeta = jnp.where(lane == TOP_K + kk, rank, rmeta)
        gmeta = jnp.where(lane == kk, exps[kk] * inv, gmeta)
    rmeta_ref[...] = rmeta.astype(I32)
    gmeta_ref[...] = gmeta
    cnt_sc[...] = cnt_sc[...] + jnp.sum(onehot, axis=0, keepdims=True)
    cnt_ref[...] = cnt_sc[...]


def _merge(ya, ym, gam, x2, wb, wo, nw, wr, br):
    t = x2.shape[0]
    tm = ROW_TILE
    row = lambda w: pl.BlockSpec((tm, w), lambda i: (i, 0))
    full = lambda a: pl.BlockSpec(a.shape, lambda i: (0,) * a.ndim)
    out_shapes = (jax.ShapeDtypeStruct((t, D_MODEL), F32), jax.ShapeDtypeStruct((t * SUBLANES, LANES), F32),
                  jax.ShapeDtypeStruct((t, LANES), I32), jax.ShapeDtypeStruct((t, LANES), F32),
                  jax.ShapeDtypeStruct((SUBLANES, LANES), F32))
    vmem = 2 * (wb.size * 2 + wo.size * 2 + wr.size * 2 + tm * D_MODEL * 22) + 16 * tm * D_MODEL * 4
    return pl.pallas_call(
        _merge_kernel, grid=(t // tm,),
        in_specs=[row(A_WIDTH), row(M_WIDTH), row(2 * D_MODEL), row(D_MODEL),
                  full(wb), full(wo), full(nw), full(wr), full(br)],
        out_specs=(row(D_MODEL), pl.BlockSpec((tm * SUBLANES, LANES), lambda i: (i, 0)), row(LANES), row(LANES),
                   pl.BlockSpec((SUBLANES, LANES), lambda i: (0, 0))),
        out_shape=out_shapes,
        scratch_shapes=[pltpu.VMEM((SUBLANES, LANES), F32)],
        compiler_params=_params(vmem), name="merge_route",
    )(ya, ym, gam, x2, wb, wo, nw, wr, br)


def _row_copy(src_ref, dst_ref, src_row, dst_row, sem):
    src = pl.multiple_of(src_row * SUBLANES, SUBLANES)
    dst = pl.multiple_of(dst_row * SUBLANES, SUBLANES)
    return pltpu.make_async_copy(src_ref.at[pl.ds(src, SUBLANES)], dst_ref.at[pl.ds(dst, SUBLANES)], sem)


def _expert_kernel(blk_e_ref, src_cur_ref, src_nxt_ref, dst_old_ref, dst_prv_ref, dst_cur_ref,
                   xn_ref, w1_ref, b1_ref, w2_ref, b2_ref, y_ref, xbuf, ybuf, w1_sc, w2_sc, gsem, ssem):
    i = pl.program_id(0)
    last = pl.num_programs(0) - 1
    fresh = jnp.logical_or(i == 0, blk_e_ref[i] != blk_e_ref[jnp.maximum(i - 1, 0)])

    def gather(idx_ref, slot):
        return [_row_copy(xn_ref, xbuf.at[slot], idx_ref[r], r, gsem.at[slot]) for r in range(MOE_BLOCK)]

    def scatter(idx_ref, slot):
        return [_row_copy(ybuf.at[slot], y_ref, r, idx_ref[r], ssem.at[slot]) for r in range(MOE_BLOCK)]

    def start_all(copies):
        for r, cp in enumerate(copies):
            cp.start(priority=r % 2)

    def wait_all(copies):
        for cp in copies:
            cp.wait()

    @pl.when(i == 0)
    def _():
        ybuf[1] = jnp.zeros(ybuf.shape[1:], F32)
        start_all(gather(src_cur_ref, 0))

    @pl.when(fresh)
    def _():
        w1_sc[...] = w1_ref[0].astype(BF16)
        w2_sc[...] = w2_ref[0].astype(BF16)

    def step(slot):
        other = 1 - slot
        wait_all(gather(src_cur_ref, slot))

        @pl.when(i >= 1)
        def _():
            wait_all(scatter(dst_old_ref, slot))

        start_all(gather(src_nxt_ref, other))
        start_all(scatter(dst_prv_ref, other))
        xb = _tiles_to_rows(xbuf.at[slot], MOE_BLOCK).astype(BF16)
        hdn = jnp.dot(xb, w1_sc[...], preferred_element_type=F32) + b1_ref[0]
        glu = jnp.minimum(hdn[:, :D_FF], SWIGLU_LIMIT)
        lin = jnp.clip(hdn[:, D_FF:], -SWIGLU_LIMIT, SWIGLU_LIMIT)
        act = glu * _sigmoid(SWIGLU_ALPHA * glu) * (lin + 1.0)
        _rows_to_tiles(ybuf.at[slot], jnp.dot(act.astype(BF16), w2_sc[...], preferred_element_type=F32) + b2_ref[0])

        @pl.when(i == last)
        def _():
            start_all(scatter(dst_cur_ref, slot))
            wait_all(gather(src_nxt_ref, other))
            wait_all(scatter(dst_prv_ref, other))
            wait_all(scatter(dst_cur_ref, slot))

    for slot in range(2):
        pl.when(i % 2 == slot)(functools.partial(step, slot))


def _expert(xn2, row_src, row_dst, blk_expert, w1, b1, w2, b2):
    n_rows = row_src.shape[0]
    n_blk = n_rows // MOE_BLOCK
    assert n_blk % 2 == 0, "the last step must use buffer slot 1 (step 0 pre-scatters from it)"
    wsel = lambda i, be: (be[i], 0, 0)
    idx = lambda off: pl.BlockSpec((MOE_BLOCK,), lambda i, be: ((i + off + n_blk) % n_blk,), memory_space=pltpu.SMEM)
    grid_spec = pltpu.PrefetchScalarGridSpec(
        num_scalar_prefetch=1, grid=(n_blk,),
        in_specs=[idx(0), idx(1), idx(-2), idx(-1), idx(0),
                  pl.BlockSpec(memory_space=pl.ANY),
                  pl.BlockSpec((1, D_MODEL, 2 * D_FF), wsel),
                  pl.BlockSpec((1, 1, 2 * D_FF), wsel),
                  pl.BlockSpec((1, D_FF, D_MODEL), wsel),
                  pl.BlockSpec((1, 1, D_MODEL), wsel)],
        out_specs=pl.BlockSpec(memory_space=pl.ANY),
        scratch_shapes=[pltpu.VMEM((2, MOE_BLOCK * SUBLANES, LANES), F32),
                        pltpu.VMEM((2, MOE_BLOCK * SUBLANES, LANES), F32),
                        pltpu.VMEM((D_MODEL, 2 * D_FF), BF16), pltpu.VMEM((D_FF, D_MODEL), BF16),
                        pltpu.SemaphoreType.DMA((2,)), pltpu.SemaphoreType.DMA((2,))])
    vmem = 2 * (D_MODEL * 2 * D_FF * 4 + D_FF * D_MODEL * 4) + 3 * D_MODEL * D_FF * 2 + 12 * MOE_BLOCK * D_MODEL * 4
    return pl.pallas_call(
        _expert_kernel, grid_spec=grid_spec,
        out_shape=jax.ShapeDtypeStruct((n_rows * SUBLANES, LANES), F32),
        compiler_params=_params(vmem), name="moe_expert",
    )(blk_expert, row_src, row_src, row_dst, row_dst, row_dst, xn2, w1, b1, w2, b2)


def _final_kernel(h_ref, g_ref, nw_ref, y0_ref, y1_ref, y2_ref, y3_ref, o_ref):
    tm = h_ref.shape[0]
    gates = g_ref[...]
    y = h_ref[...]
    for kk, yk_ref in enumerate((y0_ref, y1_ref, y2_ref, y3_ref)):
        y = y + gates[:, kk:kk + 1] * _tiles_to_rows(yk_ref, tm)
    o_ref[...] = y * lax.rsqrt(jnp.mean(y * y, axis=-1, keepdims=True) + EPS) * nw_ref[...]


def _final(h, gmeta, nw, y_slots):
    t = h.shape[0]
    tm = ROW_TILE
    steps = t // tm
    slot_spec = lambda kk: pl.BlockSpec((tm * SUBLANES, LANES), lambda i: (kk * steps + i, 0))
    return pl.pallas_call(
        _final_kernel, grid=(steps,),
        in_specs=[pl.BlockSpec((tm, D_MODEL), lambda i: (i, 0)),
                  pl.BlockSpec((tm, LANES), lambda i: (i, 0)),
                  pl.BlockSpec(nw.shape, lambda i: (0, 0))] + [slot_spec(kk) for kk in range(TOP_K)],
        out_specs=pl.BlockSpec((tm, D_MODEL), lambda i: (i, 0)),
        out_shape=jax.ShapeDtypeStruct((t, D_MODEL), F32),
        compiler_params=_params(24 * tm * D_MODEL * 4 + (4 << 20)), name="moe_combine",
    )(h, gmeta, nw, y_slots, y_slots, y_slots, y_slots)


def _rope_tables(seq):
    inv = np.float32(ROPE_THETA) ** (-np.arange(0, A_HEAD_DIM, 2, dtype=np.float32) / np.float32(A_HEAD_DIM))
    ang = np.arange(seq, dtype=np.float32)[:, None] * inv[None, :].astype(np.float32)
    cos = np.cos(ang).astype(np.float32)
    sin = np.sin(ang).astype(np.float32)
    return (jnp.asarray(np.concatenate([cos] * 4, axis=-1)),
            jnp.asarray(np.concatenate([-sin, sin, -sin, sin], axis=-1)))


def _route_plan(rmeta, cnt, n_tok):
    idx = rmeta[:, 0:TOP_K]
    rank = rmeta[:, TOP_K:2 * TOP_K]
    counts = cnt[0, :N_EXPERTS].astype(I32)
    padded = (counts + MOE_BLOCK - 1) // MOE_BLOCK * MOE_BLOCK
    pad_end = jnp.cumsum(padded)
    pad_start = pad_end - padded
    n_asg = n_tok * TOP_K
    dest = (pad_start[idx] + rank).reshape(n_asg).astype(I32)
    n_blk = n_asg // MOE_BLOCK + N_EXPERTS
    n_rows = n_blk * MOE_BLOCK
    blk_row = jnp.arange(n_blk, dtype=I32) * MOE_BLOCK
    blk_expert = jnp.minimum(jnp.sum((pad_end[None, :] <= blk_row[:, None]).astype(I32), axis=1),
                             N_EXPERTS - 1).astype(I32)
    asg = jnp.zeros((n_rows,), I32).at[dest].set(jnp.arange(1, n_asg + 1, dtype=I32), unique_indices=True)
    real = asg > 0
    a = jnp.maximum(asg - 1, 0)
    row_src = jnp.where(real, a // TOP_K, 0).astype(I32)
    spare = n_asg + jnp.cumsum(jnp.where(real, 0, 1).astype(I32)) - 1
    row_dst = jnp.where(real, (a % TOP_K) * n_tok + a // TOP_K, spare).astype(I32)
    return row_src, row_dst, blk_expert


def _layer(h3, norm_mix_w, w_in, m_conv_w, m_conv_b, m_gate_bias, m_head_norm_w, w_branch, w_out,
           norm_ffn_w, w_router, b_router, w_mlp1, b_mlp1, w_mlp2, b_mlp2, norm_out_w):
    b, s, d = h3.shape
    t = b * s
    x2 = h3.reshape(t, d)
    c_if = 3 * A_WIDTH + 4 * M_WIDTH
    w_main = jnp.concatenate(
        [w_in[:, :c_if], w_in[:, c_if + 2 * M_HEADS:],
         jnp.pad(w_in[:, c_if:c_if + 2 * M_HEADS], ((0, 0), (0, LANES - 2 * M_HEADS)))], axis=1).astype(BF16)
    gate_bias = jnp.pad(m_gate_bias, (0, LANES - 2 * M_HEADS)).reshape(1, LANES)
    cos_t, sin_t = _rope_tables(s)

    aq, ak, av, km, mqk, mv, mo, gam, gif = _inproj(
        x2, norm_mix_w.reshape(1, d), w_main, gate_bias, cos_t, sin_t, s)

    ya = _attn(aq.reshape(b, s, A_WIDTH), ak.reshape(b, s, A_WIDTH), av.reshape(b, s, 2 * A_WIDTH),
               km.reshape(b, s // MOBA_BLOCK, A_WIDTH))
    ym = _mlstm(mqk.reshape(b, s, 2 * M_WIDTH), mv.reshape(b, s, M_WIDTH), mo.reshape(b, s, M_WIDTH),
                gif.reshape(b, s, LANES), m_conv_w, m_conv_b.reshape(1, -1), m_head_norm_w.reshape(1, -1))

    wr = jnp.pad(w_router, ((0, 0), (0, LANES - N_EXPERTS)))
    wr_hi = wr.astype(BF16)
    wr = jnp.concatenate([wr_hi, (wr - wr_hi.astype(F32)).astype(BF16)], axis=1)
    br = jnp.concatenate([b_router, jnp.full((LANES - N_EXPERTS,), NEG, F32)]).reshape(1, LANES)
    h, xn2, rmeta, gmeta, cnt = _merge(
        ya.reshape(t, A_WIDTH), ym.reshape(t, M_WIDTH), gam, x2, w_branch.astype(BF16), w_out.astype(BF16),
        norm_ffn_w.reshape(1, d), wr, br)

    row_src, row_dst, blk_expert = _route_plan(rmeta, cnt, t)
    y_slots = _expert(xn2, row_src, row_dst, blk_expert, w_mlp1, b_mlp1.reshape(N_EXPERTS, 1, -1),
                      w_mlp2, b_mlp2.reshape(N_EXPERTS, 1, -1))
    out = _final(h, gmeta, norm_out_w.reshape(1, d), y_slots)
    return out.reshape(b, s, d)


def kernel(x, norm_mix_w, w_in, m_conv_w, m_conv_b, m_gate_bias, m_head_norm_w, w_branch, w_out,
           norm_ffn_w, w_router, b_router, w_mlp1, b_mlp1, w_mlp2, b_mlp2, norm_final_w):
    depth = norm_mix_w.shape[0]
    assert depth == 1, "the final RMSNorm is fused into the layer's last kernel"
    return _layer(x, norm_mix_w[0], w_in[0], m_conv_w[0], m_conv_b[0], m_gate_bias[0], m_head_norm_w[0],
                  w_branch[0], w_out[0], norm_ffn_w[0], w_router[0], b_router[0], w_mlp1[0], b_mlp1[0],
                  w_mlp2[0], b_mlp2[0], norm_final_w)
```

```python
import functools

import jax
import jax.numpy as jnp
import numpy as np
from jax import lax
from jax.experimental import pallas as pl
from jax.experimental.pallas import tpu as pltpu

F32 = jnp.float32
BF16 = jnp.bfloat16
I32 = jnp.int32
HIGHEST = lax.Precision.HIGHEST

D_MODEL = 1024
A_HEADS = 8
A_HEAD_DIM = 64
A_WIDTH = A_HEADS * A_HEAD_DIM
MOBA_BLOCK = 256
MOBA_TOPK = 3
M_HEADS = 4
M_HEAD_DIM = 128
M_WIDTH = M_HEADS * M_HEAD_DIM
M_CONV = 4
N_EXPERTS = 32
TOP_K = 4
D_FF = 1024
SWIGLU_LIMIT = 7.0
SWIGLU_ALPHA = 1.702
MOE_BLOCK = 256
ROPE_THETA = 10000.0
EPS = 1e-6
NEG = -1e30
NEG_INF = float("-inf")

LANES = 128
SUBLANES = 8
VMEM_LIMIT_CAP = 56 * 1024 * 1024

C_AQ, C_AK, C_AV = 0, 512, 1024
C_MQK, C_MV, C_MO = 1536, 2560, 3072
C_GAM, C_GIF, C_END = 3584, 5632, 5760

ROW_TILE = 256
Q_SCALE = (A_HEAD_DIM ** -0.5) * 1.4426950408889634
MLSTM_CHUNK = 256


def _params(vmem_bytes, n_axes=1):
    return pltpu.CompilerParams(
        dimension_semantics=("arbitrary",) * n_axes,
        vmem_limit_bytes=int(min(max(vmem_bytes, 16 * 1024 * 1024), VMEM_LIMIT_CAP)))


def _iota(shape, dim):
    return lax.broadcasted_iota(I32, shape, dim)


def _sigmoid(x):
    return 1.0 / (1.0 + jnp.exp(-x))


def _rows_to_tiles(ref, value):
    n = value.shape[0]
    for j in range(SUBLANES):
        ref[pl.ds(j, n, stride=SUBLANES), :] = value[:, j * LANES:(j + 1) * LANES]


def _tiles_to_rows(ref, n):
    return jnp.concatenate([ref[pl.ds(j, n, stride=SUBLANES), :] for j in range(SUBLANES)], axis=1)


def _nt_dot(a, b, precision=None):
    return lax.dot_general(a, b, (((1,), (1,)), ((), ())), precision=precision,
                           preferred_element_type=F32)


def _inproj_kernel(x_ref, nw_ref, w_ref, gb_ref, cos_ref, sin_ref,
                   aq_ref, ak_ref, av_ref, km_ref, mqk_ref, mv_ref, mo_ref, gam_ref, gif_ref):
    tm = x_ref.shape[0]
    x = x_ref[...]
    xn = x * lax.rsqrt(jnp.mean(x * x, axis=-1, keepdims=True) + EPS) * nw_ref[...]
    xb = xn.astype(BF16)

    def mm(lo, hi):
        return jnp.dot(xb, w_ref[:, lo:hi], preferred_element_type=F32)

    cos = jnp.concatenate([cos_ref[...]] * 4, axis=1)
    sin = jnp.concatenate([sin_ref[...]] * 4, axis=1)
    lane = _iota((tm, A_WIDTH), 1)
    first_half = (lane & (A_HEAD_DIM - 1)) < (A_HEAD_DIM // 2)

    def rope(t):
        up = pltpu.roll(t, A_WIDTH - A_HEAD_DIM // 2, 1)
        dn = pltpu.roll(t, A_HEAD_DIM // 2, 1)
        return t * cos + jnp.where(first_half, up, dn) * sin

    q = rope(mm(C_AQ, C_AK)) * Q_SCALE
    k = rope(mm(C_AK, C_AV))
    aq_ref[...] = q
    ak_ref[...] = k.astype(BF16)
    km_ref[0] = jnp.mean(k, axis=0, keepdims=True)

    v = mm(C_AV, C_MQK)
    lane128 = _iota((tm, LANES), 1)
    low = lane128 < A_HEAD_DIM
    for p in range(A_HEADS // 2):
        vp = v[:, p * LANES:(p + 1) * LANES]
        av_ref[:, (2 * p) * LANES:(2 * p + 1) * LANES] = jnp.where(low, vp, 1.0).astype(BF16)
        av_ref[:, (2 * p + 1) * LANES:(2 * p + 2) * LANES] = jnp.where(
            low, pltpu.roll(vp, A_HEAD_DIM, 1), 1.0).astype(BF16)

    mqk_ref[...] = mm(C_MQK, C_MV)
    mv_ref[...] = mm(C_MV, C_MO).astype(BF16)
    mo_ref[...] = mm(C_MO, C_GAM)
    gam_ref[...] = mm(C_GAM, C_GIF)
    gif_ref[...] = mm(C_GIF, C_END) + gb_ref[...]


def _inproj(x2, nw, w_main, gate_bias, cos_t, sin_t, seq):
    t = x2.shape[0]
    tm = ROW_TILE
    assert seq % tm == 0 and tm == MOBA_BLOCK
    nsteps = t // tm
    spb = seq // tm
    row = lambda w: pl.BlockSpec((tm, w), lambda i: (i, 0))
    full = lambda a: pl.BlockSpec(a.shape, lambda i: (0,) * a.ndim)
    tab = pl.BlockSpec((tm, LANES), lambda i: (i % spb, 0))
    out_shapes = (
        jax.ShapeDtypeStruct((t, A_WIDTH), F32),
        jax.ShapeDtypeStruct((t, A_WIDTH), BF16),
        jax.ShapeDtypeStruct((t, 2 * A_WIDTH), BF16),
        jax.ShapeDtypeStruct((nsteps, 1, A_WIDTH), F32),
        jax.ShapeDtypeStruct((t, 2 * M_WIDTH), F32),
        jax.ShapeDtypeStruct((t, M_WIDTH), BF16),
        jax.ShapeDtypeStruct((t, M_WIDTH), F32),
        jax.ShapeDtypeStruct((t, 2 * D_MODEL), F32),
        jax.ShapeDtypeStruct((t, LANES), F32),
    )
    out_specs = (row(A_WIDTH), row(A_WIDTH), row(2 * A_WIDTH),
                 pl.BlockSpec((1, 1, A_WIDTH), lambda i: (i, 0, 0)),
                 row(2 * M_WIDTH), row(M_WIDTH), row(M_WIDTH), row(2 * D_MODEL), row(LANES))
    vmem = 2 * (w_main.size * 2 + tm * D_MODEL * 4 + tm * C_END * 4) + 8 * tm * C_END
    return pl.pallas_call(
        _inproj_kernel, grid=(nsteps,),
        in_specs=[row(D_MODEL), full(nw), full(w_main), full(gate_bias), tab, tab],
        out_specs=out_specs, out_shape=out_shapes,
        compiler_params=_params(vmem), name="inproj",
    )(x2, nw, w_main, gate_bias, cos_t, sin_t)


def _attn_kernel(q_ref, k_ref, v_ref, km_ref, o_ref, m_sc, acc_sc, qa_sc):
    qi = pl.program_id(1)
    qc = q_ref.shape[1]
    nb = km_ref.shape[1]
    qf = q_ref[0]
    km = km_ref[0]
    kmt = jnp.concatenate([km] * A_HEADS, axis=0)
    r = _iota(kmt.shape, 0)
    c = _iota(kmt.shape, 1)
    kmt = jnp.where((r // nb) == (c // A_HEAD_DIM), kmt, 0.0)
    km_hi = kmt.astype(BF16)
    km_lo = (kmt - km_hi.astype(F32)).astype(BF16)
    q_hi = qf.astype(BF16)
    q_lo = (qf - q_hi.astype(F32)).astype(BF16)
    nrow = kmt.shape[0]
    by_hi = _nt_dot(jnp.concatenate([km_hi, km_lo], axis=0), q_hi)
    gate_t = by_hi[:nrow] + by_hi[nrow:] + _nt_dot(km_hi, q_lo)

    blk = _iota((nb, qc), 0).astype(F32)
    past = _iota((nb, qc), 0) < qi
    bias_rows = []
    for h in range(A_HEADS):
        g = jnp.where(past, gate_t[h * nb:(h + 1) * nb, :], NEG_INF)
        sel = jnp.zeros((nb, qc), F32)
        for _ in range(MOBA_TOPK):
            top = jnp.max(g, axis=0, keepdims=True)
            first = jnp.min(jnp.where(g == top, blk, float(nb)), axis=0, keepdims=True)
            hit = jnp.logical_and(blk == first, top > NEG_INF)
            sel = jnp.where(hit, 1.0, sel)
            g = jnp.where(hit, NEG_INF, g)
        bias_rows.append(jnp.where(sel > 0.0, 0.0, NEG))
    if A_HEADS * nb < LANES:
        bias_rows.append(jnp.zeros((LANES - A_HEADS * nb, qc), F32))
    bias = jnp.concatenate(bias_rows, axis=0).T

    lane = _iota((qc, LANES), 1)
    klane = _iota((MOBA_BLOCK, LANES), 1)
    causal = _iota((qc, MOBA_BLOCK), 1) <= _iota((qc, MOBA_BLOCK), 0)
    own = pl.multiple_of(qi * MOBA_BLOCK, MOBA_BLOCK)

    for h in range(A_HEADS):
        ksl = slice((h // 2) * LANES, (h // 2 + 1) * LANES)
        qh = jnp.where((lane // A_HEAD_DIM) == (h % 2), qf[:, ksl], 0.0).astype(BF16)
        bh = jnp.where((lane // nb) == h, bias, 0.0).astype(BF16)
        qa_sc[h] = jnp.concatenate([qh, bh], axis=1)
        s = jnp.where(causal, _nt_dot(qh, k_ref[0, pl.ds(own, MOBA_BLOCK), ksl]), NEG)
        m0 = jnp.max(s, axis=1, keepdims=True)
        pr = jnp.exp2(s - m0)
        m_sc[h] = jnp.broadcast_to(m0, (qc, LANES))
        acc_sc[h] = jnp.dot(pr.astype(BF16), v_ref[0, pl.ds(own, MOBA_BLOCK), h * LANES:(h + 1) * LANES],
                            preferred_element_type=F32)

    def block(n):
        start = pl.multiple_of(n * MOBA_BLOCK, MOBA_BLOCK)
        onehot = jnp.where((klane % nb) == n, 1.0, 0.0).astype(BF16)
        for p in range(A_HEADS // 2):
            k_aug = jnp.concatenate([k_ref[0, pl.ds(start, MOBA_BLOCK), p * LANES:(p + 1) * LANES], onehot], axis=1)
            for h in (2 * p, 2 * p + 1):
                s = _nt_dot(qa_sc[h], k_aug)
                m_prev = m_sc[h]
                m_new = jnp.maximum(m_prev, jnp.max(s, axis=1, keepdims=True))
                alpha = jnp.exp2(m_prev - m_new)
                pr = jnp.exp2(s - jnp.concatenate([m_new, m_new], axis=1))
                pv = jnp.dot(pr.astype(BF16), v_ref[0, pl.ds(start, MOBA_BLOCK), h * LANES:(h + 1) * LANES],
                             preferred_element_type=F32)
                acc_sc[h] = alpha * acc_sc[h] + pv
                m_sc[h] = m_new

    def body(n4, carry):
        for j in range(4):
            block(4 * n4 + j)
        return carry

    lax.fori_loop(0, qi // 4, body, 0)
    rem = qi % 4
    done = qi - rem

    @pl.when(rem >= 2)
    def _():
        block(done)
        block(done + 1)

    @pl.when(rem % 2 == 1)
    def _():
        block(qi - 1)

    for p in range(A_HEADS // 2):
        a0 = acc_sc[2 * p]
        a1 = acc_sc[2 * p + 1]
        o0 = a0 / pltpu.roll(a0, A_HEAD_DIM, 1)
        o1 = a1 / pltpu.roll(a1, A_HEAD_DIM, 1)
        o_ref[0, :, p * LANES:(p + 1) * LANES] = jnp.where(
            lane < A_HEAD_DIM, o0, pltpu.roll(o1, A_HEAD_DIM, 1)).astype(BF16)


def _attn(q, k, v, km):
    b, s, _ = q.shape
    nb = s // MOBA_BLOCK
    assert nb * A_HEADS <= LANES, "block-bias columns must fit one lane group"
    qc = MOBA_BLOCK
    vmem = 2 * (s * A_WIDTH * 2 + s * 2 * A_WIDTH * 2 + qc * A_WIDTH * 6) + 16 * qc * 256 * 4 + (4 << 20)
    return pl.pallas_call(
        _attn_kernel, grid=(b, s // qc),
        in_specs=[pl.BlockSpec((1, qc, A_WIDTH), lambda bi, i: (bi, i, 0)),
                  pl.BlockSpec((1, s, A_WIDTH), lambda bi, i: (bi, 0, 0)),
                  pl.BlockSpec((1, s, 2 * A_WIDTH), lambda bi, i: (bi, 0, 0)),
                  pl.BlockSpec((1, nb, A_WIDTH), lambda bi, i: (bi, 0, 0))],
        out_specs=pl.BlockSpec((1, qc, A_WIDTH), lambda bi, i: (bi, i, 0)),
        out_shape=jax.ShapeDtypeStruct((b, s, A_WIDTH), BF16),
        scratch_shapes=[pltpu.VMEM((A_HEADS, qc, LANES), F32), pltpu.VMEM((A_HEADS, qc, LANES), F32),
                        pltpu.VMEM((A_HEADS, qc, 2 * LANES), BF16)],
        compiler_params=_params(vmem, 2), name="moba_attn",
    )(q, k, v, km)


def _mlstm_kernel(qk_ref, v_ref, o_ref, g_ref, cw_ref, cb_ref, hw_ref, y_ref, ext_sc, c_sc, m_sc):
    ci = pl.program_id(1)
    ln = qk_ref.shape[1]
    dh = M_HEAD_DIM

    @pl.when(ci == 0)
    def _():
        ext_sc[0:SUBLANES, :] = jnp.zeros((SUBLANES, 2 * M_WIDTH), F32)
        c_sc[...] = jnp.zeros(c_sc.shape, F32)
        m_sc[...] = jnp.zeros(m_sc.shape, F32)

    u = qk_ref[0]
    ext_sc[SUBLANES:SUBLANES + ln, :] = u
    conv = cb_ref[...]
    for j in range(M_CONV):
        conv = conv + cw_ref[j:j + 1, :] * ext_sc[pl.ds(SUBLANES - (M_CONV - 1) + j, ln), :]
    ext_sc[0:SUBLANES, :] = u[ln - SUBLANES:ln, :]
    act = conv * _sigmoid(conv)

    gates = g_ref[0]
    log_f = jnp.minimum(gates, 0.0) - jnp.log(1.0 + jnp.exp(-jnp.abs(gates)))
    row = _iota((ln, ln), 0)
    col = _iota((ln, ln), 1)
    causal = col <= row
    lf_hi = log_f.astype(BF16)
    lf_r = log_f - lf_hi.astype(F32)
    lf_mid = lf_r.astype(BF16)
    lf_lo = (lf_r - lf_mid.astype(F32)).astype(BF16)
    tri = causal.astype(BF16)
    two = jnp.dot(tri, jnp.concatenate([lf_hi, lf_mid], axis=1), preferred_element_type=F32)
    b_cols = two[:, :LANES] + two[:, LANES:] + jnp.dot(tri, lf_lo, preferred_element_type=F32)
    gates_t = gates.T
    b_rows = b_cols.T
    ones = jnp.ones((ln, dh), BF16)

    for h in range(M_HEADS):
        hs = slice(h * dh, (h + 1) * dh)
        qh = act[:, hs].astype(BF16)
        kh = act[:, M_WIDTH + h * dh:M_WIDTH + (h + 1) * dh] * (dh ** -0.5)
        b_col = b_cols[:, M_HEADS + h:M_HEADS + h + 1]
        b_row = b_rows[M_HEADS + h:M_HEADS + h + 1, :]
        i_col = gates[:, h:h + 1]
        i_row = gates_t[h:h + 1, :]
        b_last = b_col[ln - 1:ln, :]
        m_st = m_sc[h][0:1, 0:1]

        d = jnp.where(causal, b_col - b_row + i_row, NEG_INF)
        g = b_col + m_st
        m_t = jnp.maximum(g, jnp.max(d, axis=1, keepdims=True))
        w_intra = jnp.exp(d - m_t)
        w_inter = jnp.exp(g - m_t)
        qk = (_nt_dot(qh, kh.astype(BF16)) * w_intra).astype(BF16)
        v_aug = jnp.concatenate([v_ref[0, :, hs], ones], axis=1)
        c_aug = c_sc[h]
        res = (w_inter * jnp.dot(qh, c_aug.astype(BF16), preferred_element_type=F32)
               + jnp.dot(qk, v_aug, preferred_element_type=F32))
        num = res[:, :dh]
        den = res[:, dh:]
        h_t = num / jnp.maximum(jnp.abs(den), jnp.exp(-m_t))

        m_new = jnp.maximum(b_last + m_st, jnp.max(b_last - b_row + i_row, axis=1, keepdims=True))
        w_k = jnp.exp(b_last - b_col + i_col - m_new)
        decay = jnp.exp(b_last + m_st - m_new)
        kw_t = (kh * w_k).T.astype(BF16)
        c_sc[h] = decay * c_aug + jnp.dot(kw_t, v_aug, preferred_element_type=F32)
        m_sc[h] = jnp.broadcast_to(m_new, (SUBLANES, LANES))

        hn = h_t * lax.rsqrt(jnp.mean(h_t * h_t, axis=1, keepdims=True) + EPS) * hw_ref[:, hs]
        y_ref[0, :, hs] = (hn * _sigmoid(o_ref[0, :, hs])).astype(BF16)


def _mlstm(mqk, mv, mo, gif, conv_w, conv_b, head_w):
    b, s, _ = mqk.shape
    ln = MLSTM_CHUNK
    assert s % ln == 0
    blk = lambda w: pl.BlockSpec((1, ln, w), lambda bi, i: (bi, i, 0))
    full = lambda a: pl.BlockSpec(a.shape, lambda bi, i: (0,) * a.ndim)
    vmem = 2 * ln * (2 * M_WIDTH * 4 + M_WIDTH * 10 + LANES * 4) + 24 * ln * ln * 4 + (8 << 20)
    return pl.pallas_call(
        _mlstm_kernel, grid=(b, s // ln),
        in_specs=[blk(2 * M_WIDTH), blk(M_WIDTH), blk(M_WIDTH), blk(LANES),
                  full(conv_w), full(conv_b), full(head_w)],
        out_specs=blk(M_WIDTH),
        out_shape=jax.ShapeDtypeStruct((b, s, M_WIDTH), BF16),
        scratch_shapes=[pltpu.VMEM((ln + SUBLANES, 2 * M_WIDTH), F32),
                        pltpu.VMEM((M_HEADS, M_HEAD_DIM, 2 * M_HEAD_DIM), F32),
                        pltpu.VMEM((M_HEADS, SUBLANES, LANES), F32)],
        compiler_params=_params(vmem, 2), name="mlstm",
    )(mqk, mv, mo, gif, conv_w, conv_b, head_w)


def _merge_kernel(ya_ref, ym_ref, gam_ref, x_ref, wb_ref, wo_ref, nw_ref, wr_ref, br_ref,
                  h_ref, xn_ref, rmeta_ref, gmeta_ref, cnt_ref, cnt_sc):
    i = pl.program_id(0)
    tm = x_ref.shape[0]

    @pl.when(i == 0)
    def _():
        cnt_sc[...] = jnp.zeros(cnt_sc.shape, F32)

    pa = jnp.dot(ya_ref[...], wb_ref[0:A_WIDTH, :], preferred_element_type=F32)
    pm = jnp.dot(ym_ref[...], wb_ref[A_WIDTH:, :], preferred_element_type=F32)
    merged = _sigmoid(gam_ref[:, 0:D_MODEL]) * pa + _sigmoid(gam_ref[:, D_MODEL:]) * pm
    h = x_ref[...] + jnp.dot(merged.astype(BF16), wo_ref[...], preferred_element_type=F32)
    h_ref[...] = h
    xn = h * lax.rsqrt(jnp.mean(h * h, axis=-1, keepdims=True) + EPS) * nw_ref[...]
    _rows_to_tiles(xn_ref, xn)

    x_hi = xn.astype(BF16)
    x_lo = (xn - x_hi.astype(F32)).astype(BF16)
    both = jnp.dot(x_hi, wr_ref[...], preferred_element_type=F32)
    logits = (both[:, :LANES] + both[:, LANES:]
              + jnp.dot(x_lo, wr_ref[:, :LANES], preferred_element_type=F32) + br_ref[...])
    lane = _iota((tm, LANES), 1)
    lane_f = lane.astype(F32)
    cur = logits
    vals, idxs, hits = [], [], []
    for _ in range(TOP_K):
        top = jnp.max(cur, axis=1, keepdims=True)
        first = jnp.min(jnp.where(cur == top, lane_f, float(LANES)), axis=1, keepdims=True)
        hit = lane_f == first
        vals.append(top)
        idxs.append(first)
        hits.append(hit)
        cur = jnp.where(hit, NEG_INF, cur)
    exps = [jnp.exp(v - vals[0]) for v in vals]
    inv = 1.0 / (exps[0] + exps[1] + exps[2] + exps[3])

    onehot = jnp.zeros((tm, LANES), F32)
    for hit in hits:
        onehot = jnp.where(hit, 1.0, onehot)
    before = (_iota((tm, tm), 1) < _iota((tm, tm), 0)).astype(BF16)
    seen = jnp.dot(before, onehot.astype(BF16), preferred_element_type=F32) + cnt_sc[0:1, :]
    rmeta = jnp.zeros((tm, LANES), F32)
    gmeta = jnp.zeros((tm, LANES), F32)
    for kk in range(TOP_K):
        rank = jnp.sum(jnp.where(hits[kk], seen, 0.0), axis=1, keepdims=True)
        rmeta = jnp.where(lane == kk, idxs[kk], rmeta)
        rmeta = jnp.where(lane == TOP_K + kk, rank, rmeta)
        gmeta = jnp.where(lane == kk, exps[kk] * inv, gmeta)
    rmeta_ref[...] = rmeta.astype(I32)
    gmeta_ref[...] = gmeta
    cnt_sc[...] = cnt_sc[...] + jnp.sum(onehot, axis=0, keepdims=True)
    cnt_ref[...] = cnt_sc[...]


def _merge(ya, ym, gam, x2, wb, wo, nw, wr, br):
    t = x2.shape[0]
    tm = ROW_TILE
    row = lambda w: pl.BlockSpec((tm, w), lambda i: (i, 0))
    full = lambda a: pl.BlockSpec(a.shape, lambda i: (0,) * a.ndim)
    out_shapes = (jax.ShapeDtypeStruct((t, D_MODEL), F32), jax.ShapeDtypeStruct((t * SUBLANES, LANES), F32),
                  jax.ShapeDtypeStruct((t, LANES), I32), jax.ShapeDtypeStruct((t, LANES), F32),
                  jax.ShapeDtypeStruct((SUBLANES, LANES), F32))
    vmem = 2 * (wb.size * 2 + wo.size * 2 + wr.size * 2 + tm * D_MODEL * 22) + 16 * tm * D_MODEL * 4
    return pl.pallas_call(
        _merge_kernel, grid=(t // tm,),
        in_specs=[row(A_WIDTH), row(M_WIDTH), row(2 * D_MODEL), row(D_MODEL),
                  full(wb), full(wo), full(nw), full(wr), full(br)],
        out_specs=(row(D_MODEL), pl.BlockSpec((tm * SUBLANES, LANES), lambda i: (i, 0)), row(LANES), row(LANES),
                   pl.BlockSpec((SUBLANES, LANES), lambda i: (0, 0))),
        out_shape=out_shapes,
        scratch_shapes=[pltpu.VMEM((SUBLANES, LANES), F32)],
        compiler_params=_params(vmem), name="merge_route",
    )(ya, ym, gam, x2, wb, wo, nw, wr, br)


def _row_copy(src_ref, dst_ref, src_row, dst_row, sem):
    src = pl.multiple_of(src_row * SUBLANES, SUBLANES)
    dst = pl.multiple_of(dst_row * SUBLANES, SUBLANES)
    return pltpu.make_async_copy(src_ref.at[pl.ds(src, SUBLANES)], dst_ref.at[pl.ds(dst, SUBLANES)], sem)


def _expert_kernel(blk_e_ref, src_cur_ref, src_nxt_ref, dst_old_ref, dst_prv_ref, dst_cur_ref,
                   xn_ref, w1_ref, b1_ref, w2_ref, b2_ref, y_ref, xbuf, ybuf, w1_sc, w2_sc, gsem, ssem):
    i = pl.program_id(0)
    last = pl.num_programs(0) - 1
    fresh = jnp.logical_or(i == 0, blk_e_ref[i] != blk_e_ref[jnp.maximum(i - 1, 0)])

    def gather(idx_ref, slot):
        return [_row_copy(xn_ref, xbuf.at[slot], idx_ref[r], r, gsem.at[slot]) for r in range(MOE_BLOCK)]

    def scatter(idx_ref, slot):
        return [_row_copy(ybuf.at[slot], y_ref, r, idx_ref[r], ssem.at[slot]) for r in range(MOE_BLOCK)]

    def start_all(copies, thread):
        for cp in copies:
            cp.start(priority=thread)

    def wait_all(copies):
        for cp in copies:
            cp.wait()

    @pl.when(i == 0)
    def _():
        ybuf[1] = jnp.zeros(ybuf.shape[1:], F32)
        start_all(gather(src_cur_ref, 0), 0)

    @pl.when(fresh)
    def _():
        w1_sc[...] = w1_ref[0].astype(BF16)
        w2_sc[...] = w2_ref[0].astype(BF16)

    def step(slot):
        other = 1 - slot
        wait_all(gather(src_cur_ref, slot))

        @pl.when(i >= 1)
        def _():
            wait_all(scatter(dst_old_ref, slot))

        start_all(gather(src_nxt_ref, other), 0)
        start_all(scatter(dst_prv_ref, other), 1)
        xb = _tiles_to_rows(xbuf.at[slot], MOE_BLOCK).astype(BF16)
        hdn = jnp.dot(xb, w1_sc[...], preferred_element_type=F32) + b1_ref[0]
        glu = jnp.minimum(hdn[:, :D_FF], SWIGLU_LIMIT)
        lin = jnp.clip(hdn[:, D_FF:], -SWIGLU_LIMIT, SWIGLU_LIMIT)
        act = glu * _sigmoid(SWIGLU_ALPHA * glu) * (lin + 1.0)
        _rows_to_tiles(ybuf.at[slot], jnp.dot(act.astype(BF16), w2_sc[...], preferred_element_type=F32) + b2_ref[0])

        @pl.when(i == last)
        def _():
            start_all(scatter(dst_cur_ref, slot), 1)
            wait_all(gather(src_nxt_ref, other))
            wait_all(scatter(dst_prv_ref, other))
            wait_all(scatter(dst_cur_ref, slot))

    for slot in range(2):
        pl.when(i % 2 == slot)(functools.partial(step, slot))


def _expert(xn2, row_src, row_dst, blk_expert, w1, b1, w2, b2):
    n_rows = row_src.shape[0]
    n_blk = n_rows // MOE_BLOCK
    assert n_blk % 2 == 0, "the last step must use buffer slot 1 (step 0 pre-scatters from it)"
    wsel = lambda i, be: (be[i], 0, 0)
    idx = lambda off: pl.BlockSpec((MOE_BLOCK,), lambda i, be: ((i + off + n_blk) % n_blk,), memory_space=pltpu.SMEM)
    grid_spec = pltpu.PrefetchScalarGridSpec(
        num_scalar_prefetch=1, grid=(n_blk,),
        in_specs=[idx(0), idx(1), idx(-2), idx(-1), idx(0),
                  pl.BlockSpec(memory_space=pl.ANY),
                  pl.BlockSpec((1, D_MODEL, 2 * D_FF), wsel),
                  pl.BlockSpec((1, 1, 2 * D_FF), wsel),
                  pl.BlockSpec((1, D_FF, D_MODEL), wsel),
                  pl.BlockSpec((1, 1, D_MODEL), wsel)],
        out_specs=pl.BlockSpec(memory_space=pl.ANY),
        scratch_shapes=[pltpu.VMEM((2, MOE_BLOCK * SUBLANES, LANES), F32),
                        pltpu.VMEM((2, MOE_BLOCK * SUBLANES, LANES), F32),
                        pltpu.VMEM((D_MODEL, 2 * D_FF), BF16), pltpu.VMEM((D_FF, D_MODEL), BF16),
                        pltpu.SemaphoreType.DMA((2,)), pltpu.SemaphoreType.DMA((2,))])
    vmem = 2 * (D_MODEL * 2 * D_FF * 4 + D_FF * D_MODEL * 4) + 3 * D_MODEL * D_FF * 2 + 12 * MOE_BLOCK * D_MODEL * 4
    return pl.pallas_call(
        _expert_kernel, grid_spec=grid_spec,
        out_shape=jax.ShapeDtypeStruct((n_rows * SUBLANES, LANES), F32),
        compiler_params=_params(vmem), name="moe_expert",
    )(blk_expert, row_src, row_src, row_dst, row_dst, row_dst, xn2, w1, b1, w2, b2)


def _final_kernel(h_ref, g_ref, nw_ref, y0_ref, y1_ref, y2_ref, y3_ref, o_ref):
    tm = h_ref.shape[0]
    gates = g_ref[...]
    y = h_ref[...]
    for kk, yk_ref in enumerate((y0_ref, y1_ref, y2_ref, y3_ref)):
        y = y + gates[:, kk:kk + 1] * _tiles_to_rows(yk_ref, tm)
    o_ref[...] = y * lax.rsqrt(jnp.mean(y * y, axis=-1, keepdims=True) + EPS) * nw_ref[...]


def _final(h, gmeta, nw, y_slots):
    t = h.shape[0]
    tm = ROW_TILE
    steps = t // tm
    slot_spec = lambda kk: pl.BlockSpec((tm * SUBLANES, LANES), lambda i: (kk * steps + i, 0))
    return pl.pallas_call(
        _final_kernel, grid=(steps,),
        in_specs=[pl.BlockSpec((tm, D_MODEL), lambda i: (i, 0)),
                  pl.BlockSpec((tm, LANES), lambda i: (i, 0)),
                  pl.BlockSpec(nw.shape, lambda i: (0, 0))] + [slot_spec(kk) for kk in range(TOP_K)],
        out_specs=pl.BlockSpec((tm, D_MODEL), lambda i: (i, 0)),
        out_shape=jax.ShapeDtypeStruct((t, D_MODEL), F32),
        compiler_params=_params(24 * tm * D_MODEL * 4 + (4 << 20)), name="moe_combine",
    )(h, gmeta, nw, y_slots, y_slots, y_slots, y_slots)


def _rope_tables(seq):
    inv = np.float32(ROPE_THETA) ** (-np.arange(0, A_HEAD_DIM, 2, dtype=np.float32) / np.float32(A_HEAD_DIM))
    ang = np.arange(seq, dtype=np.float32)[:, None] * inv[None, :].astype(np.float32)
    cos = np.cos(ang).astype(np.float32)
    sin = np.sin(ang).astype(np.float32)
    return (jnp.asarray(np.concatenate([cos] * 4, axis=-1)),
            jnp.asarray(np.concatenate([-sin, sin, -sin, sin], axis=-1)))


def _route_plan(rmeta, cnt, n_tok):
    idx = rmeta[:, 0:TOP_K]
    rank = rmeta[:, TOP_K:2 * TOP_K]
    counts = cnt[0, :N_EXPERTS].astype(I32)
    padded = (counts + MOE_BLOCK - 1) // MOE_BLOCK * MOE_BLOCK
    pad_end = jnp.cumsum(padded)
    pad_start = pad_end - padded
    n_asg = n_tok * TOP_K
    dest = (pad_start[idx] + rank).reshape(n_asg).astype(I32)
    n_blk = n_asg // MOE_BLOCK + N_EXPERTS
    n_rows = n_blk * MOE_BLOCK
    blk_row = jnp.arange(n_blk, dtype=I32) * MOE_BLOCK
    blk_expert = jnp.minimum(jnp.sum((pad_end[None, :] <= blk_row[:, None]).astype(I32), axis=1),
                             N_EXPERTS - 1).astype(I32)
    asg = jnp.zeros((n_rows,), I32).at[dest].set(jnp.arange(1, n_asg + 1, dtype=I32), unique_indices=True)
    real = asg > 0
    a = jnp.maximum(asg - 1, 0)
    row_src = jnp.where(real, a // TOP_K, 0).astype(I32)
    spare = n_asg + jnp.cumsum(jnp.where(real, 0, 1).astype(I32)) - 1
    row_dst = jnp.where(real, (a % TOP_K) * n_tok + a // TOP_K, spare).astype(I32)
    return row_src, row_dst, blk_expert


def _layer(h3, norm_mix_w, w_in, m_conv_w, m_conv_b, m_gate_bias, m_head_norm_w, w_branch, w_out,
           norm_ffn_w, w_router, b_router, w_mlp1, b_mlp1, w_mlp2, b_mlp2, norm_out_w):
    b, s, d = h3.shape
    t = b * s
    x2 = h3.reshape(t, d)
    c_if = 3 * A_WIDTH + 4 * M_WIDTH
    w_main = jnp.concatenate(
        [w_in[:, :c_if], w_in[:, c_if + 2 * M_HEADS:],
         jnp.pad(w_in[:, c_if:c_if + 2 * M_HEADS], ((0, 0), (0, LANES - 2 * M_HEADS)))], axis=1).astype(BF16)
    gate_bias = jnp.pad(m_gate_bias, (0, LANES - 2 * M_HEADS)).reshape(1, LANES)
    cos_t, sin_t = _rope_tables(s)

    aq, ak, av, km, mqk, mv, mo, gam, gif = _inproj(
        x2, norm_mix_w.reshape(1, d), w_main, gate_bias, cos_t, sin_t, s)

    ya = _attn(aq.reshape(b, s, A_WIDTH), ak.reshape(b, s, A_WIDTH), av.reshape(b, s, 2 * A_WIDTH),
               km.reshape(b, s // MOBA_BLOCK, A_WIDTH))
    ym = _mlstm(mqk.reshape(b, s, 2 * M_WIDTH), mv.reshape(b, s, M_WIDTH), mo.reshape(b, s, M_WIDTH),
                gif.reshape(b, s, LANES), m_conv_w, m_conv_b.reshape(1, -1), m_head_norm_w.reshape(1, -1))

    wr = jnp.pad(w_router, ((0, 0), (0, LANES - N_EXPERTS)))
    wr_hi = wr.astype(BF16)
    wr = jnp.concatenate([wr_hi, (wr - wr_hi.astype(F32)).astype(BF16)], axis=1)
    br = jnp.concatenate([b_router, jnp.full((LANES - N_EXPERTS,), NEG, F32)]).reshape(1, LANES)
    h, xn2, rmeta, gmeta, cnt = _merge(
        ya.reshape(t, A_WIDTH), ym.reshape(t, M_WIDTH), gam, x2, w_branch.astype(BF16), w_out.astype(BF16),
        norm_ffn_w.reshape(1, d), wr, br)

    row_src, row_dst, blk_expert = _route_plan(rmeta, cnt, t)
    y_slots = _expert(xn2, row_src, row_dst, blk_expert, w_mlp1, b_mlp1.reshape(N_EXPERTS, 1, -1),
                      w_mlp2, b_mlp2.reshape(N_EXPERTS, 1, -1))
    out = _final(h, gmeta, norm_out_w.reshape(1, d), y_slots)
    return out.reshape(b, s, d)


def kernel(x, norm_mix_w, w_in, m_conv_w, m_conv_b, m_gate_bias, m_head_norm_w, w_branch, w_out,
           norm_ffn_w, w_router, b_router, w_mlp1, b_mlp1, w_mlp2, b_mlp2, norm_final_w):
    depth = norm_mix_w.shape[0]
    assert depth == 1, "the final RMSNorm is fused into the layer's last kernel"
    return _layer(x, norm_mix_w[0], w_in[0], m_conv_w[0], m_conv_b[0], m_gate_bias[0], m_head_norm_w[0],
                  w_branch[0], w_out[0], norm_ffn_w[0], w_router[0], b_router[0], w_mlp1[0], b_mlp1[0],
                  w_mlp2[0], b_mlp2[0], norm_final_w)
```

```python
import functools

import jax
import jax.numpy as jnp
import numpy as np
from jax import lax
from jax.experimental import pallas as pl
from jax.experimental.pallas import tpu as pltpu

F32 = jnp.float32
BF16 = jnp.bfloat16
I32 = jnp.int32
HIGHEST = lax.Precision.HIGHEST

D_MODEL = 1024
A_HEADS = 8
A_HEAD_DIM = 64
A_WIDTH = A_HEADS * A_HEAD_DIM
MOBA_BLOCK = 256
MOBA_TOPK = 3
M_HEADS = 4
M_HEAD_DIM = 128
M_WIDTH = M_HEADS * M_HEAD_DIM
M_CONV = 4
N_EXPERTS = 32
TOP_K = 4
D_FF = 1024
SWIGLU_LIMIT = 7.0
SWIGLU_ALPHA = 1.702
MOE_BLOCK = 256
ROPE_THETA = 10000.0
EPS = 1e-6
NEG = -1e30
NEG_INF = float("-inf")

LANES = 128
SUBLANES = 8
VMEM_LIMIT_CAP = 56 * 1024 * 1024

C_AQ, C_AK, C_AV = 0, 512, 1024
C_MQK, C_MV, C_MO = 1536, 2560, 3072
C_GAM, C_GIF, C_END = 3584, 5632, 5760

ROW_TILE = 256
Q_SCALE = (A_HEAD_DIM ** -0.5) * 1.4426950408889634
MLSTM_CHUNK = 256


def _params(vmem_bytes, n_axes=1):
    return pltpu.CompilerParams(
        dimension_semantics=("arbitrary",) * n_axes,
        vmem_limit_bytes=int(min(max(vmem_bytes, 16 * 1024 * 1024), VMEM_LIMIT_CAP)))


def _iota(shape, dim):
    return lax.broadcasted_iota(I32, shape, dim)


def _sigmoid(x):
    return 1.0 / (1.0 + jnp.exp(-x))


def _rows_to_tiles(ref, value):
    n = value.shape[0]
    for j in range(SUBLANES):
        ref[pl.ds(j, n, stride=SUBLANES), :] = value[:, j * LANES:(j + 1) * LANES]


def _tiles_to_rows(ref, n):
    return jnp.concatenate([ref[pl.ds(j, n, stride=SUBLANES), :] for j in range(SUBLANES)], axis=1)


def _nt_dot(a, b, precision=None):
    return lax.dot_general(a, b, (((1,), (1,)), ((), ())), precision=precision,
                           preferred_element_type=F32)


def _inproj_kernel(x_ref, nw_ref, w_ref, gb_ref, cos_ref, sin_ref,
                   aq_ref, ak_ref, av_ref, km_ref, mqk_ref, mv_ref, mo_ref, gam_ref, gif_ref):
    tm = x_ref.shape[0]
    x = x_ref[...]
    xn = x * lax.rsqrt(jnp.mean(x * x, axis=-1, keepdims=True) + EPS) * nw_ref[...]
    xb = xn.astype(BF16)

    def mm(lo, hi):
        return jnp.dot(xb, w_ref[:, lo:hi], preferred_element_type=F32)

    cos = jnp.concatenate([cos_ref[...]] * 4, axis=1)
    sin = jnp.concatenate([sin_ref[...]] * 4, axis=1)
    lane = _iota((tm, A_WIDTH), 1)
    first_half = (lane & (A_HEAD_DIM - 1)) < (A_HEAD_DIM // 2)

    def rope(t):
        up = pltpu.roll(t, A_WIDTH - A_HEAD_DIM // 2, 1)
        dn = pltpu.roll(t, A_HEAD_DIM // 2, 1)
        return t * cos + jnp.where(first_half, up, dn) * sin

    q = rope(mm(C_AQ, C_AK)) * Q_SCALE
    k = rope(mm(C_AK, C_AV))
    aq_ref[...] = q
    ak_ref[...] = k.astype(BF16)
    km_ref[0] = jnp.mean(k, axis=0, keepdims=True)

    v = mm(C_AV, C_MQK)
    lane128 = _iota((tm, LANES), 1)
    low = lane128 < A_HEAD_DIM
    for p in range(A_HEADS // 2):
        vp = v[:, p * LANES:(p + 1) * LANES]
        av_ref[:, (2 * p) * LANES:(2 * p + 1) * LANES] = jnp.where(low, vp, 1.0).astype(BF16)
        av_ref[:, (2 * p + 1) * LANES:(2 * p + 2) * LANES] = jnp.where(
            low, pltpu.roll(vp, A_HEAD_DIM, 1), 1.0).astype(BF16)

    mqk_ref[...] = mm(C_MQK, C_MV)
    mv_ref[...] = mm(C_MV, C_MO).astype(BF16)
    mo_ref[...] = mm(C_MO, C_GAM)
    gam_ref[...] = mm(C_GAM, C_GIF)
    gif_ref[...] = mm(C_GIF, C_END) + gb_ref[...]


def _inproj(x2, nw, w_main, gate_bias, cos_t, sin_t, seq):
    t = x2.shape[0]
    tm = ROW_TILE
    assert seq % tm == 0 and tm == MOBA_BLOCK
    nsteps = t // tm
    spb = seq // tm
    row = lambda w: pl.BlockSpec((tm, w), lambda i: (i, 0))
    full = lambda a: pl.BlockSpec(a.shape, lambda i: (0,) * a.ndim)
    tab = pl.BlockSpec((tm, LANES), lambda i: (i % spb, 0))
    out_shapes = (
        jax.ShapeDtypeStruct((t, A_WIDTH), F32),
        jax.ShapeDtypeStruct((t, A_WIDTH), BF16),
        jax.ShapeDtypeStruct((t, 2 * A_WIDTH), BF16),
        jax.ShapeDtypeStruct((nsteps, 1, A_WIDTH), F32),
        jax.ShapeDtypeStruct((t, 2 * M_WIDTH), F32),
        jax.ShapeDtypeStruct((t, M_WIDTH), BF16),
        jax.ShapeDtypeStruct((t, M_WIDTH), F32),
        jax.ShapeDtypeStruct((t, 2 * D_MODEL), F32),
        jax.ShapeDtypeStruct((t, LANES), F32),
    )
    out_specs = (row(A_WIDTH), row(A_WIDTH), row(2 * A_WIDTH),
                 pl.BlockSpec((1, 1, A_WIDTH), lambda i: (i, 0, 0)),
                 row(2 * M_WIDTH), row(M_WIDTH), row(M_WIDTH), row(2 * D_MODEL), row(LANES))
    vmem = 2 * (w_main.size * 2 + tm * D_MODEL * 4 + tm * C_END * 4) + 8 * tm * C_END
    return pl.pallas_call(
        _inproj_kernel, grid=(nsteps,),
        in_specs=[row(D_MODEL), full(nw), full(w_main), full(gate_bias), tab, tab],
        out_specs=out_specs, out_shape=out_shapes,
        compiler_params=_params(vmem), name="inproj",
    )(x2, nw, w_main, gate_bias, cos_t, sin_t)


def _attn_kernel(q_ref, k_ref, v_ref, km_ref, o_ref, m_sc, acc_sc, qa_sc):
    qi = pl.program_id(1)
    qc = q_ref.shape[1]
    nb = km_ref.shape[1]
    qf = q_ref[0]
    km = km_ref[0]
    kmt = jnp.concatenate([km] * A_HEADS, axis=0)
    r = _iota(kmt.shape, 0)
    c = _iota(kmt.shape, 1)
    kmt = jnp.where((r // nb) == (c // A_HEAD_DIM), kmt, 0.0)
    km_hi = kmt.astype(BF16)
    km_lo = (kmt - km_hi.astype(F32)).astype(BF16)
    q_hi = qf.astype(BF16)
    q_lo = (qf - q_hi.astype(F32)).astype(BF16)
    nrow = kmt.shape[0]
    by_hi = _nt_dot(jnp.concatenate([km_hi, km_lo], axis=0), q_hi)
    gate_t = by_hi[:nrow] + by_hi[nrow:] + _nt_dot(km_hi, q_lo)

    blk = _iota((nb, qc), 0).astype(F32)
    past = _iota((nb, qc), 0) < qi
    bias_rows = []
    for h in range(A_HEADS):
        g = jnp.where(past, gate_t[h * nb:(h + 1) * nb, :], NEG_INF)
        sel = jnp.zeros((nb, qc), F32)
        for _ in range(MOBA_TOPK):
            top = jnp.max(g, axis=0, keepdims=True)
            first = jnp.min(jnp.where(g == top, blk, float(nb)), axis=0, keepdims=True)
            hit = jnp.logical_and(blk == first, top > NEG_INF)
            sel = jnp.where(hit, 1.0, sel)
            g = jnp.where(hit, NEG_INF, g)
        bias_rows.append(jnp.where(sel > 0.0, 0.0, NEG))
    if A_HEADS * nb < LANES:
        bias_rows.append(jnp.zeros((LANES - A_HEADS * nb, qc), F32))
    bias = jnp.concatenate(bias_rows, axis=0).T

    lane = _iota((qc, LANES), 1)
    klane = _iota((MOBA_BLOCK, LANES), 1)
    causal = _iota((qc, MOBA_BLOCK), 1) <= _iota((qc, MOBA_BLOCK), 0)
    own = pl.multiple_of(qi * MOBA_BLOCK, MOBA_BLOCK)

    for h in range(A_HEADS):
        ksl = slice((h // 2) * LANES, (h // 2 + 1) * LANES)
        qh = jnp.where((lane // A_HEAD_DIM) == (h % 2), qf[:, ksl], 0.0).astype(BF16)
        bh = jnp.where((lane // nb) == h, bias, 0.0).astype(BF16)
        qa_sc[h] = jnp.concatenate([qh, bh], axis=1)
        s = jnp.where(causal, _nt_dot(qh, k_ref[0, pl.ds(own, MOBA_BLOCK), ksl]), NEG)
        m0 = jnp.max(s, axis=1, keepdims=True)
        pr = jnp.exp2(s - m0)
        m_sc[h] = jnp.broadcast_to(m0, (qc, LANES))
        acc_sc[h] = jnp.dot(pr.astype(BF16), v_ref[0, pl.ds(own, MOBA_BLOCK), h * LANES:(h + 1) * LANES],
                            preferred_element_type=F32)

    def block(n):
        start = pl.multiple_of(n * MOBA_BLOCK, MOBA_BLOCK)
        onehot = jnp.where((klane % nb) == n, 1.0, 0.0).astype(BF16)
        for p in range(A_HEADS // 2):
            k_aug = jnp.concatenate([k_ref[0, pl.ds(start, MOBA_BLOCK), p * LANES:(p + 1) * LANES], onehot], axis=1)
            for h in (2 * p, 2 * p + 1):
                s = _nt_dot(qa_sc[h], k_aug)
                m_prev = m_sc[h]
                m_new = jnp.maximum(m_prev, jnp.max(s, axis=1, keepdims=True))
                alpha = jnp.exp2(m_prev - m_new)
                pr = jnp.exp2(s - jnp.concatenate([m_new, m_new], axis=1))
                pv = jnp.dot(pr.astype(BF16), v_ref[0, pl.ds(start, MOBA_BLOCK), h * LANES:(h + 1) * LANES],
                             preferred_element_type=F32)
                acc_sc[h] = alpha * acc_sc[h] + pv
                m_sc[h] = m_new

    def body(n4, carry):
        for j in range(4):
            block(4 * n4 + j)
        return carry

    lax.fori_loop(0, qi // 4, body, 0)
    rem = qi % 4
    done = qi - rem

    @pl.when(rem >= 2)
    def _():
        block(done)
        block(done + 1)

    @pl.when(rem % 2 == 1)
    def _():
        block(qi - 1)

    for p in range(A_HEADS // 2):
        a0 = acc_sc[2 * p]
        a1 = acc_sc[2 * p + 1]
        o0 = a0 / pltpu.roll(a0, A_HEAD_DIM, 1)
        o1 = a1 / pltpu.roll(a1, A_HEAD_DIM, 1)
        o_ref[0, :, p * LANES:(p + 1) * LANES] = jnp.where(
            lane < A_HEAD_DIM, o0, pltpu.roll(o1, A_HEAD_DIM, 1)).astype(BF16)


def _attn(q, k, v, km):
    b, s, _ = q.shape
    nb = s // MOBA_BLOCK
    assert nb * A_HEADS <= LANES, "block-bias columns must fit one lane group"
    qc = MOBA_BLOCK
    vmem = 2 * (s * A_WIDTH * 2 + s * 2 * A_WIDTH * 2 + qc * A_WIDTH * 6) + 16 * qc * 256 * 4 + (4 << 20)
    return pl.pallas_call(
        _attn_kernel, grid=(b, s // qc),
        in_specs=[pl.BlockSpec((1, qc, A_WIDTH), lambda bi, i: (bi, i, 0)),
                  pl.BlockSpec((1, s, A_WIDTH), lambda bi, i: (bi, 0, 0)),
                  pl.BlockSpec((1, s, 2 * A_WIDTH), lambda bi, i: (bi, 0, 0)),
                  pl.BlockSpec((1, nb, A_WIDTH), lambda bi, i: (bi, 0, 0))],
        out_specs=pl.BlockSpec((1, qc, A_WIDTH), lambda bi, i: (bi, i, 0)),
        out_shape=jax.ShapeDtypeStruct((b, s, A_WIDTH), BF16),
        scratch_shapes=[pltpu.VMEM((A_HEADS, qc, LANES), F32), pltpu.VMEM((A_HEADS, qc, LANES), F32),
                        pltpu.VMEM((A_HEADS, qc, 2 * LANES), BF16)],
        compiler_params=_params(vmem, 2), name="moba_attn",
    )(q, k, v, km)


def _mlstm_kernel(qk_ref, v_ref, o_ref, g_ref, cw_ref, cb_ref, hw_ref, y_ref, ext_sc, c_sc, m_sc):
    ci = pl.program_id(1)
    ln = qk_ref.shape[1]
    dh = M_HEAD_DIM

    @pl.when(ci == 0)
    def _():
        ext_sc[0:SUBLANES, :] = jnp.zeros((SUBLANES, 2 * M_WIDTH), F32)
        c_sc[...] = jnp.zeros(c_sc.shape, F32)
        m_sc[...] = jnp.zeros(m_sc.shape, F32)

    u = qk_ref[0]
    ext_sc[SUBLANES:SUBLANES + ln, :] = u
    conv = cb_ref[...]
    for j in range(M_CONV):
        conv = conv + cw_ref[j:j + 1, :] * ext_sc[pl.ds(SUBLANES - (M_CONV - 1) + j, ln), :]
    ext_sc[0:SUBLANES, :] = u[ln - SUBLANES:ln, :]
    act = conv * _sigmoid(conv)

    gates = g_ref[0]
    log_f = jnp.minimum(gates, 0.0) - jnp.log(1.0 + jnp.exp(-jnp.abs(gates)))
    row = _iota((ln, ln), 0)
    col = _iota((ln, ln), 1)
    causal = col <= row
    lf_hi = log_f.astype(BF16)
    lf_r = log_f - lf_hi.astype(F32)
    lf_mid = lf_r.astype(BF16)
    lf_lo = (lf_r - lf_mid.astype(F32)).astype(BF16)
    tri = causal.astype(BF16)
    two = jnp.dot(tri, jnp.concatenate([lf_hi, lf_mid], axis=1), preferred_element_type=F32)
    b_cols = two[:, :LANES] + two[:, LANES:] + jnp.dot(tri, lf_lo, preferred_element_type=F32)
    gates_t = gates.T
    b_rows = b_cols.T
    ones = jnp.ones((ln, dh), BF16)

    for h in range(M_HEADS):
        hs = slice(h * dh, (h + 1) * dh)
        qh = act[:, hs].astype(BF16)
        kh = act[:, M_WIDTH + h * dh:M_WIDTH + (h + 1) * dh] * (dh ** -0.5)
        b_col = b_cols[:, M_HEADS + h:M_HEADS + h + 1]
        b_row = b_rows[M_HEADS + h:M_HEADS + h + 1, :]
        i_col = gates[:, h:h + 1]
        i_row = gates_t[h:h + 1, :]
        b_last = b_col[ln - 1:ln, :]
        m_st = m_sc[h][0:1, 0:1]

        d = jnp.where(causal, b_col - b_row + i_row, NEG_INF)
        g = b_col + m_st
        m_t = jnp.maximum(g, jnp.max(d, axis=1, keepdims=True))
        w_intra = jnp.exp(d - m_t)
        w_inter = jnp.exp(g - m_t)
        qk = (_nt_dot(qh, kh.astype(BF16)) * w_intra).astype(BF16)
        v_aug = jnp.concatenate([v_ref[0, :, hs], ones], axis=1)
        c_aug = c_sc[h]
        res = (w_inter * jnp.dot(qh, c_aug.astype(BF16), preferred_element_type=F32)
               + jnp.dot(qk, v_aug, preferred_element_type=F32))
        num = res[:, :dh]
        den = res[:, dh:]
        h_t = num / jnp.maximum(jnp.abs(den), jnp.exp(-m_t))

        m_new = jnp.maximum(b_last + m_st, jnp.max(b_last - b_row + i_row, axis=1, keepdims=True))
        w_k = jnp.exp(b_last - b_col + i_col - m_new)
        decay = jnp.exp(b_last + m_st - m_new)
        kw_t = (kh * w_k).T.astype(BF16)
        c_sc[h] = decay * c_aug + jnp.dot(kw_t, v_aug, preferred_element_type=F32)
        m_sc[h] = jnp.broadcast_to(m_new, (SUBLANES, LANES))

        hn = h_t * lax.rsqrt(jnp.mean(h_t * h_t, axis=1, keepdims=True) + EPS) * hw_ref[:, hs]
        y_ref[0, :, hs] = (hn * _sigmoid(o_ref[0, :, hs])).astype(BF16)


def _mlstm(mqk, mv, mo, gif, conv_w, conv_b, head_w):
    b, s, _ = mqk.shape
    ln = MLSTM_CHUNK
    assert s % ln == 0
    blk = lambda w: pl.BlockSpec((1, ln, w), lambda bi, i: (bi, i, 0))
    full = lambda a: pl.BlockSpec(a.shape, lambda bi, i: (0,) * a.ndim)
    vmem = 2 * ln * (2 * M_WIDTH * 4 + M_WIDTH * 10 + LANES * 4) + 24 * ln * ln * 4 + (8 << 20)
    return pl.pallas_call(
        _mlstm_kernel, grid=(b, s // ln),
        in_specs=[blk(2 * M_WIDTH), blk(M_WIDTH), blk(M_WIDTH), blk(LANES),
                  full(conv_w), full(conv_b), full(head_w)],
        out_specs=blk(M_WIDTH),
        out_shape=jax.ShapeDtypeStruct((b, s, M_WIDTH), BF16),
        scratch_shapes=[pltpu.VMEM((ln + SUBLANES, 2 * M_WIDTH), F32),
                        pltpu.VMEM((M_HEADS, M_HEAD_DIM, 2 * M_HEAD_DIM), F32),
                        pltpu.VMEM((M_HEADS, SUBLANES, LANES), F32)],
        compiler_params=_params(vmem, 2), name="mlstm",
    )(mqk, mv, mo, gif, conv_w, conv_b, head_w)


def _merge_kernel(ya_ref, ym_ref, gam_ref, x_ref, wb_ref, wo_ref, nw_ref, wr_ref, br_ref,
                  h_ref, xn_ref, rmeta_ref, gmeta_ref, cnt_ref, cnt_sc):
    i = pl.program_id(0)
    tm = x_ref.shape[0]

    @pl.when(i == 0)
    def _():
        cnt_sc[...] = jnp.zeros(cnt_sc.shape, F32)

    pa = jnp.dot(ya_ref[...], wb_ref[0:A_WIDTH, :], preferred_element_type=F32)
    pm = jnp.dot(ym_ref[...], wb_ref[A_WIDTH:, :], preferred_element_type=F32)
    merged = _sigmoid(gam_ref[:, 0:D_MODEL]) * pa + _sigmoid(gam_ref[:, D_MODEL:]) * pm
    h = x_ref[...] + jnp.dot(merged.astype(BF16), wo_ref[...], preferred_element_type=F32)
    h_ref[...] = h
    xn = h * lax.rsqrt(jnp.mean(h * h, axis=-1, keepdims=True) + EPS) * nw_ref[...]
    _rows_to_tiles(xn_ref, xn)

    x_hi = xn.astype(BF16)
    x_lo = (xn - x_hi.astype(F32)).astype(BF16)
    both = jnp.dot(x_hi, wr_ref[...], preferred_element_type=F32)
    logits = (both[:, :LANES] + both[:, LANES:]
              + jnp.dot(x_lo, wr_ref[:, :LANES], preferred_element_type=F32) + br_ref[...])
    lane = _iota((tm, LANES), 1)
    lane_f = lane.astype(F32)
    cur = logits
    vals, idxs, hits = [], [], []
    for _ in range(TOP_K):
        top = jnp.max(cur, axis=1, keepdims=True)
        first = jnp.min(jnp.where(cur == top, lane_f, float(LANES)), axis=1, keepdims=True)
        hit = lane_f == first
        vals.append(top)
        idxs.append(first)
        hits.append(hit)
        cur = jnp.where(hit, NEG_INF, cur)
    exps = [jnp.exp(v - vals[0]) for v in vals]
    inv = 1.0 / (exps[0] + exps[1] + exps[2] + exps[3])

    onehot = jnp.zeros((tm, LANES), F32)
    for hit in hits:
        onehot = jnp.where(hit, 1.0, onehot)
    before = (_iota((tm, tm), 1) < _iota((tm, tm), 0)).astype(BF16)
    seen = jnp.dot(before, onehot.astype(BF16), preferred_element_type=F32) + cnt_sc[0:1, :]
    rmeta = jnp.zeros((tm, LANES), F32)
    gmeta = jnp.zeros((tm, LANES), F32)
    for kk in range(TOP_K):
        rank = jnp.sum(jnp.where(hits[kk], seen, 0.0), axis=1, keepdims=True)
        rmeta = jnp.where(lane == kk, idxs[kk], rmeta)
        rmeta = jnp.where(lane == TOP_K + kk, rank, rmeta)
        gmeta = jnp.where(lane == kk, exps[kk] * inv, gmeta)
    rmeta_ref[...] = rmeta.astype(I32)
    gmeta_ref[...] = gmeta
    cnt_sc[...] = cnt_sc[...] + jnp.sum(onehot, axis=0, keepdims=True)
    cnt_ref[...] = cnt_sc[...]


def _merge(ya, ym, gam, x2, wb, wo, nw, wr, br):
    t = x2.shape[0]
    tm = ROW_TILE
    row = lambda w: pl.BlockSpec((tm, w), lambda i: (i, 0))
    full = lambda a: pl.BlockSpec(a.shape, lambda i: (0,) * a.ndim)
    out_shapes = (jax.ShapeDtypeStruct((t, D_MODEL), F32), jax.ShapeDtypeStruct((t * SUBLANES, LANES), F32),
                  jax.ShapeDtypeStruct((t, LANES), I32), jax.ShapeDtypeStruct((t, LANES), F32),
                  jax.ShapeDtypeStruct((SUBLANES, LANES), F32))
    vmem = 2 * (wb.size * 2 + wo.size * 2 + wr.size * 2 + tm * D_MODEL * 22) + 16 * tm * D_MODEL * 4
    return pl.pallas_call(
        _merge_kernel, grid=(t // tm,),
        in_specs=[row(A_WIDTH), row(M_WIDTH), row(2 * D_MODEL), row(D_MODEL),
                  full(wb), full(wo), full(nw), full(wr), full(br)],
        out_specs=(row(D_MODEL), pl.BlockSpec((tm * SUBLANES, LANES), lambda i: (i, 0)), row(LANES), row(LANES),
                   pl.BlockSpec((SUBLANES, LANES), lambda i: (0, 0))),
        out_shape=out_shapes,
        scratch_shapes=[pltpu.VMEM((SUBLANES, LANES), F32)],
        compiler_params=_params(vmem), name="merge_route",
    )(ya, ym, gam, x2, wb, wo, nw, wr, br)


def _row_copy(src_ref, dst_ref, src_row, dst_row, sem):
    src = pl.multiple_of(src_row * SUBLANES, SUBLANES)
    dst = pl.multiple_of(dst_row * SUBLANES, SUBLANES)
    return pltpu.make_async_copy(src_ref.at[pl.ds(src, SUBLANES)], dst_ref.at[pl.ds(dst, SUBLANES)], sem)


def _expert_kernel(blk_e_ref, src_cur_ref, src_nxt_ref, dst_old_ref, dst_prv_ref, dst_cur_ref,
                   xn_ref, w1_ref, b1_ref, w2_ref, b2_ref, y_ref, xbuf, ybuf, w1_sc, w2_sc, gsem, ssem):
    i = pl.program_id(0)
    last = pl.num_programs(0) - 1
    fresh = jnp.logical_or(i == 0, blk_e_ref[i] != blk_e_ref[jnp.maximum(i - 1, 0)])

    def gather(idx_ref, slot):
        return [_row_copy(xn_ref, xbuf.at[slot], idx_ref[r], r, gsem.at[slot]) for r in range(MOE_BLOCK)]

    def scatter(idx_ref, slot):
        return [_row_copy(ybuf.at[slot], y_ref, r, idx_ref[r], ssem.at[slot]) for r in range(MOE_BLOCK)]

    def start_all(copies, thread):
        for cp in copies:
            cp.start(priority=thread)

    def wait_all(copies):
        for cp in copies:
            cp.wait()

    @pl.when(i == 0)
    def _():
        ybuf[1] = jnp.zeros(ybuf.shape[1:], F32)
        start_all(gather(src_cur_ref, 0), 0)

    @pl.when(fresh)
    def _():
        w1_sc[...] = w1_ref[0].astype(BF16)
        w2_sc[...] = w2_ref[0].astype(BF16)

    def step(slot):
        other = 1 - slot
        wait_all(gather(src_cur_ref, slot))

        @pl.when(i >= 1)
        def _():
            wait_all(scatter(dst_old_ref, slot))

        start_all(gather(src_nxt_ref, other), 0)
        start_all(scatter(dst_prv_ref, other), 1)
        xb = _tiles_to_rows(xbuf.at[slot], MOE_BLOCK).astype(BF16)
        hdn = jnp.dot(xb, w1_sc[...], preferred_element_type=F32) + b1_ref[0]
        glu = jnp.minimum(hdn[:, :D_FF], SWIGLU_LIMIT)
        lin = jnp.clip(hdn[:, D_FF:], -SWIGLU_LIMIT, SWIGLU_LIMIT)
        act = glu * _sigmoid(SWIGLU_ALPHA * glu) * (lin + 1.0)
        _rows_to_tiles(ybuf.at[slot], jnp.dot(act.astype(BF16), w2_sc[...], preferred_element_type=F32) + b2_ref[0])

        @pl.when(i == last)
        def _():
            start_all(scatter(dst_cur_ref, slot), 1)
            wait_all(gather(src_nxt_ref, other))
            wait_all(scatter(dst_prv_ref, other))
            wait_all(scatter(dst_cur_ref, slot))

    for slot in range(2):
        pl.when(i % 2 == slot)(functools.partial(step, slot))


def _expert(xn2, row_src, row_dst, blk_expert, w1, b1, w2, b2):
    n_rows = row_src.shape[0]
    n_blk = n_rows // MOE_BLOCK
    assert n_blk % 2 == 0, "the last step must use buffer slot 1 (step 0 pre-scatters from it)"
    wsel = lambda i, be: (be[i], 0, 0)
    idx = lambda off: pl.BlockSpec((MOE_BLOCK,), lambda i, be: ((i + off + n_blk) % n_blk,), memory_space=pltpu.SMEM)
    grid_spec = pltpu.PrefetchScalarGridSpec(
        num_scalar_prefetch=1, grid=(n_blk,),
        in_specs=[idx(0), idx(1), idx(-2), idx(-1), idx(0),
                  pl.BlockSpec(memory_space=pl.ANY),
                  pl.BlockSpec((1, D_MODEL, 2 * D_FF), wsel),
                  pl.BlockSpec((1, 1, 2 * D_FF), wsel),
                  pl.BlockSpec((1, D_FF, D_MODEL), wsel),
                  pl.BlockSpec((1, 1, D_MODEL), wsel)],
        out_specs=pl.BlockSpec(memory_space=pl.ANY),
        scratch_shapes=[pltpu.VMEM((2, MOE_BLOCK * SUBLANES, LANES), F32),
                        pltpu.VMEM((2, MOE_BLOCK * SUBLANES, LANES), F32),
                        pltpu.VMEM((D_MODEL, 2 * D_FF), BF16), pltpu.VMEM((D_FF, D_MODEL), BF16),
                        pltpu.SemaphoreType.DMA((2,)), pltpu.SemaphoreType.DMA((2,))])
    vmem = 2 * (D_MODEL * 2 * D_FF * 4 + D_FF * D_MODEL * 4) + 3 * D_MODEL * D_FF * 2 + 12 * MOE_BLOCK * D_MODEL * 4
    return pl.pallas_call(
        _expert_kernel, grid_spec=grid_spec,
        out_shape=jax.ShapeDtypeStruct((n_rows * SUBLANES, LANES), F32),
        compiler_params=_params(vmem), name="moe_expert",
    )(blk_expert, row_src, row_src, row_dst, row_dst, row_dst, xn2, w1, b1, w2, b2)


INV_BLOCK = 8192
SCALAR_UNROLL = 8


def _invmap_kernel(dest_ref, inv_ref):
    i = pl.program_id(0)
    blk = dest_ref.shape[0]

    @pl.when(i == 0)
    def _():
        def clear(r, carry):
            inv_ref[r] = 0
            return carry

        lax.fori_loop(0, inv_ref.shape[0], clear, 0, unroll=SCALAR_UNROLL)

    def put(j, carry):
        inv_ref[dest_ref[j]] = i * blk + j + 1
        return carry

    lax.fori_loop(0, blk, put, 0, unroll=SCALAR_UNROLL)


def _invmap(dest, n_rows):
    n_asg = dest.shape[0]
    blk = INV_BLOCK if n_asg % INV_BLOCK == 0 else n_asg
    return pl.pallas_call(
        _invmap_kernel, grid=(n_asg // blk,),
        in_specs=[pl.BlockSpec((blk,), lambda i: (i,), memory_space=pltpu.SMEM)],
        out_specs=pl.BlockSpec((n_rows,), lambda i: (0,), memory_space=pltpu.SMEM),
        out_shape=jax.ShapeDtypeStruct((n_rows,), I32),
        compiler_params=_params(0), name="moe_invmap",
    )(dest)


def _final_kernel(h_ref, g_ref, nw_ref, y0_ref, y1_ref, y2_ref, y3_ref, o_ref):
    tm = h_ref.shape[0]
    gates = g_ref[...]
    y = h_ref[...]
    for kk, yk_ref in enumerate((y0_ref, y1_ref, y2_ref, y3_ref)):
        y = y + gates[:, kk:kk + 1] * _tiles_to_rows(yk_ref, tm)
    o_ref[...] = y * lax.rsqrt(jnp.mean(y * y, axis=-1, keepdims=True) + EPS) * nw_ref[...]


def _final(h, gmeta, nw, y_slots):
    t = h.shape[0]
    tm = ROW_TILE
    steps = t // tm
    slot_spec = lambda kk: pl.BlockSpec((tm * SUBLANES, LANES), lambda i: (kk * steps + i, 0))
    return pl.pallas_call(
        _final_kernel, grid=(steps,),
        in_specs=[pl.BlockSpec((tm, D_MODEL), lambda i: (i, 0)),
                  pl.BlockSpec((tm, LANES), lambda i: (i, 0)),
                  pl.BlockSpec(nw.shape, lambda i: (0, 0))] + [slot_spec(kk) for kk in range(TOP_K)],
        out_specs=pl.BlockSpec((tm, D_MODEL), lambda i: (i, 0)),
        out_shape=jax.ShapeDtypeStruct((t, D_MODEL), F32),
        compiler_params=_params(24 * tm * D_MODEL * 4 + (4 << 20)), name="moe_combine",
    )(h, gmeta, nw, y_slots, y_slots, y_slots, y_slots)


def _rope_tables(seq):
    inv = np.float32(ROPE_THETA) ** (-np.arange(0, A_HEAD_DIM, 2, dtype=np.float32) / np.float32(A_HEAD_DIM))
    ang = np.arange(seq, dtype=np.float32)[:, None] * inv[None, :].astype(np.float32)
    cos = np.cos(ang).astype(np.float32)
    sin = np.sin(ang).astype(np.float32)
    return (jnp.asarray(np.concatenate([cos] * 4, axis=-1)),
            jnp.asarray(np.concatenate([-sin, sin, -sin, sin], axis=-1)))


def _route_plan(rmeta, cnt, n_tok):
    idx = rmeta[:, 0:TOP_K]
    rank = rmeta[:, TOP_K:2 * TOP_K]
    counts = cnt[0, :N_EXPERTS].astype(I32)
    padded = (counts + MOE_BLOCK - 1) // MOE_BLOCK * MOE_BLOCK
    pad_end = jnp.cumsum(padded)
    pad_start = pad_end - padded
    n_asg = n_tok * TOP_K
    dest = (pad_start[idx] + rank).reshape(n_asg).astype(I32)
    n_blk = n_asg // MOE_BLOCK + N_EXPERTS
    n_rows = n_blk * MOE_BLOCK
    blk_row = jnp.arange(n_blk, dtype=I32) * MOE_BLOCK
    blk_expert = jnp.minimum(jnp.sum((pad_end[None, :] <= blk_row[:, None]).astype(I32), axis=1),
                             N_EXPERTS - 1).astype(I32)
    asg = _invmap(dest, n_rows)
    real = asg > 0
    a = jnp.maximum(asg - 1, 0)
    row_src = jnp.where(real, a // TOP_K, 0).astype(I32)
    spare = n_asg + jnp.cumsum(jnp.where(real, 0, 1).astype(I32)) - 1
    row_dst = jnp.where(real, (a % TOP_K) * n_tok + a // TOP_K, spare).astype(I32)
    return row_src, row_dst, blk_expert


def _layer(h3, norm_mix_w, w_in, m_conv_w, m_conv_b, m_gate_bias, m_head_norm_w, w_branch, w_out,
           norm_ffn_w, w_router, b_router, w_mlp1, b_mlp1, w_mlp2, b_mlp2, norm_out_w):
    b, s, d = h3.shape
    t = b * s
    x2 = h3.reshape(t, d)
    c_if = 3 * A_WIDTH + 4 * M_WIDTH
    w_main = jnp.concatenate(
        [w_in[:, :c_if], w_in[:, c_if + 2 * M_HEADS:],
         jnp.pad(w_in[:, c_if:c_if + 2 * M_HEADS], ((0, 0), (0, LANES - 2 * M_HEADS)))], axis=1).astype(BF16)
    gate_bias = jnp.pad(m_gate_bias, (0, LANES - 2 * M_HEADS)).reshape(1, LANES)
    cos_t, sin_t = _rope_tables(s)

    aq, ak, av, km, mqk, mv, mo, gam, gif = _inproj(
        x2, norm_mix_w.reshape(1, d), w_main, gate_bias, cos_t, sin_t, s)

    ya = _attn(aq.reshape(b, s, A_WIDTH), ak.reshape(b, s, A_WIDTH), av.reshape(b, s, 2 * A_WIDTH),
               km.reshape(b, s // MOBA_BLOCK, A_WIDTH))
    ym = _mlstm(mqk.reshape(b, s, 2 * M_WIDTH), mv.reshape(b, s, M_WIDTH), mo.reshape(b, s, M_WIDTH),
                gif.reshape(b, s, LANES), m_conv_w, m_conv_b.reshape(1, -1), m_head_norm_w.reshape(1, -1))

    wr = jnp.pad(w_router, ((0, 0), (0, LANES - N_EXPERTS)))
    wr_hi = wr.astype(BF16)
    wr = jnp.concatenate([wr_hi, (wr - wr_hi.astype(F32)).astype(BF16)], axis=1)
    br = jnp.concatenate([b_router, jnp.full((LANES - N_EXPERTS,), NEG, F32)]).reshape(1, LANES)
    h, xn2, rmeta, gmeta, cnt = _merge(
        ya.reshape(t, A_WIDTH), ym.reshape(t, M_WIDTH), gam, x2, w_branch.astype(BF16), w_out.astype(BF16),
        norm_ffn_w.reshape(1, d), wr, br)

    row_src, row_dst, blk_expert = _route_plan(rmeta, cnt, t)
    y_slots = _expert(xn2, row_src, row_dst, blk_expert, w_mlp1, b_mlp1.reshape(N_EXPERTS, 1, -1),
                      w_mlp2, b_mlp2.reshape(N_EXPERTS, 1, -1))
    out = _final(h, gmeta, norm_out_w.reshape(1, d), y_slots)
    return out.reshape(b, s, d)


def kernel(x, norm_mix_w, w_in, m_conv_w, m_conv_b, m_gate_bias, m_head_norm_w, w_branch, w_out,
           norm_ffn_w, w_router, b_router, w_mlp1, b_mlp1, w_mlp2, b_mlp2, norm_final_w):
    depth = norm_mix_w.shape[0]
    assert depth == 1, "the final RMSNorm is fused into the layer's last kernel"
    return _layer(x, norm_mix_w[0], w_in[0], m_conv_w[0], m_conv_b[0], m_gate_bias[0], m_head_norm_w[0],
                  w_branch[0], w_out[0], norm_ffn_w[0], w_router[0], b_router[0], w_mlp1[0], b_mlp1[0],
                  w_mlp2[0], b_mlp2[0], norm_final_w)
```

```python
import functools

import jax
import jax.numpy as jnp
import numpy as np
from jax import lax
from jax.experimental import pallas as pl
from jax.experimental.pallas import tpu as pltpu

F32 = jnp.float32
BF16 = jnp.bfloat16
I32 = jnp.int32
HIGHEST = lax.Precision.HIGHEST

D_MODEL = 1024
A_HEADS = 8
A_HEAD_DIM = 64
A_WIDTH = A_HEADS * A_HEAD_DIM
MOBA_BLOCK = 256
MOBA_TOPK = 3
M_HEADS = 4
M_HEAD_DIM = 128
M_WIDTH = M_HEADS * M_HEAD_DIM
M_CONV = 4
N_EXPERTS = 32
TOP_K = 4
D_FF = 1024
SWIGLU_LIMIT = 7.0
SWIGLU_ALPHA = 1.702
MOE_BLOCK = 256
ROPE_THETA = 10000.0
EPS = 1e-6
NEG = -1e30
NEG_INF = float("-inf")

LANES = 128
SUBLANES = 8
VMEM_LIMIT_CAP = 56 * 1024 * 1024

C_AQ, C_AK, C_AV = 0, 512, 1024
C_MQK, C_MV, C_MO = 1536, 2560, 3072
C_GAM, C_GIF, C_END = 3584, 5632, 5760

ROW_TILE = 256
Q_SCALE = (A_HEAD_DIM ** -0.5) * 1.4426950408889634
MLSTM_CHUNK = 256


def _params(vmem_bytes, n_axes=1):
    return pltpu.CompilerParams(
        dimension_semantics=("arbitrary",) * n_axes,
        vmem_limit_bytes=int(min(max(vmem_bytes, 16 * 1024 * 1024), VMEM_LIMIT_CAP)))


def _iota(shape, dim):
    return lax.broadcasted_iota(I32, shape, dim)


def _sigmoid(x):
    return 1.0 / (1.0 + jnp.exp(-x))


def _rows_to_tiles(ref, value):
    n = value.shape[0]
    for j in range(SUBLANES):
        ref[pl.ds(j, n, stride=SUBLANES), :] = value[:, j * LANES:(j + 1) * LANES]


def _tiles_to_rows(ref, n):
    return jnp.concatenate([ref[pl.ds(j, n, stride=SUBLANES), :] for j in range(SUBLANES)], axis=1)


def _nt_dot(a, b, precision=None):
    return lax.dot_general(a, b, (((1,), (1,)), ((), ())), precision=precision,
                           preferred_element_type=F32)


def _inproj_kernel(x_ref, nw_ref, w_ref, gb_ref, cos_ref, sin_ref,
                   aq_ref, ak_ref, av_ref, km_ref, mqk_ref, mv_ref, mo_ref, gam_ref, gif_ref):
    tm = x_ref.shape[0]
    x = x_ref[...]
    xn = x * lax.rsqrt(jnp.mean(x * x, axis=-1, keepdims=True) + EPS) * nw_ref[...]
    xb = xn.astype(BF16)

    def mm(lo, hi):
        return jnp.dot(xb, w_ref[:, lo:hi], preferred_element_type=F32)

    cos = jnp.concatenate([cos_ref[...]] * 4, axis=1)
    sin = jnp.concatenate([sin_ref[...]] * 4, axis=1)
    lane = _iota((tm, A_WIDTH), 1)
    first_half = (lane & (A_HEAD_DIM - 1)) < (A_HEAD_DIM // 2)

    def rope(t):
        up = pltpu.roll(t, A_WIDTH - A_HEAD_DIM // 2, 1)
        dn = pltpu.roll(t, A_HEAD_DIM // 2, 1)
        return t * cos + jnp.where(first_half, up, dn) * sin

    q = rope(mm(C_AQ, C_AK)) * Q_SCALE
    k = rope(mm(C_AK, C_AV))
    aq_ref[...] = q
    ak_ref[...] = k.astype(BF16)
    km_ref[0] = jnp.mean(k, axis=0, keepdims=True)

    v = mm(C_AV, C_MQK)
    lane128 = _iota((tm, LANES), 1)
    low = lane128 < A_HEAD_DIM
    for p in range(A_HEADS // 2):
        vp = v[:, p * LANES:(p + 1) * LANES]
        av_ref[:, (2 * p) * LANES:(2 * p + 1) * LANES] = jnp.where(low, vp, 1.0).astype(BF16)
        av_ref[:, (2 * p + 1) * LANES:(2 * p + 2) * LANES] = jnp.where(
            low, pltpu.roll(vp, A_HEAD_DIM, 1), 1.0).astype(BF16)

    mqk_ref[...] = mm(C_MQK, C_MV)
    mv_ref[...] = mm(C_MV, C_MO).astype(BF16)
    mo_ref[...] = mm(C_MO, C_GAM)
    gam_ref[...] = mm(C_GAM, C_GIF)
    gif_ref[...] = mm(C_GIF, C_END) + gb_ref[...]


def _inproj(x2, nw, w_main, gate_bias, cos_t, sin_t, seq):
    t = x2.shape[0]
    tm = ROW_TILE
    assert seq % tm == 0 and tm == MOBA_BLOCK
    nsteps = t // tm
    spb = seq // tm
    row = lambda w: pl.BlockSpec((tm, w), lambda i: (i, 0))
    full = lambda a: pl.BlockSpec(a.shape, lambda i: (0,) * a.ndim)
    tab = pl.BlockSpec((tm, LANES), lambda i: (i % spb, 0))
    out_shapes = (
        jax.ShapeDtypeStruct((t, A_WIDTH), F32),
        jax.ShapeDtypeStruct((t, A_WIDTH), BF16),
        jax.ShapeDtypeStruct((t, 2 * A_WIDTH), BF16),
        jax.ShapeDtypeStruct((nsteps, 1, A_WIDTH), F32),
        jax.ShapeDtypeStruct((t, 2 * M_WIDTH), F32),
        jax.ShapeDtypeStruct((t, M_WIDTH), BF16),
        jax.ShapeDtypeStruct((t, M_WIDTH), F32),
        jax.ShapeDtypeStruct((t, 2 * D_MODEL), F32),
        jax.ShapeDtypeStruct((t, LANES), F32),
    )
    out_specs = (row(A_WIDTH), row(A_WIDTH), row(2 * A_WIDTH),
                 pl.BlockSpec((1, 1, A_WIDTH), lambda i: (i, 0, 0)),
                 row(2 * M_WIDTH), row(M_WIDTH), row(M_WIDTH), row(2 * D_MODEL), row(LANES))
    vmem = 2 * (w_main.size * 2 + tm * D_MODEL * 4 + tm * C_END * 4) + 8 * tm * C_END
    return pl.pallas_call(
        _inproj_kernel, grid=(nsteps,),
        in_specs=[row(D_MODEL), full(nw), full(w_main), full(gate_bias), tab, tab],
        out_specs=out_specs, out_shape=out_shapes,
        compiler_params=_params(vmem), name="inproj",
    )(x2, nw, w_main, gate_bias, cos_t, sin_t)


def _attn_kernel(q_ref, k_ref, v_ref, km_ref, o_ref, m_sc, acc_sc, qa_sc):
    qi = pl.program_id(1)
    qc = q_ref.shape[1]
    nb = km_ref.shape[1]
    qf = q_ref[0]
    km = km_ref[0]
    kmt = jnp.concatenate([km] * A_HEADS, axis=0)
    r = _iota(kmt.shape, 0)
    c = _iota(kmt.shape, 1)
    kmt = jnp.where((r // nb) == (c // A_HEAD_DIM), kmt, 0.0)
    km_hi = kmt.astype(BF16)
    km_lo = (kmt - km_hi.astype(F32)).astype(BF16)
    q_hi = qf.astype(BF16)
    q_lo = (qf - q_hi.astype(F32)).astype(BF16)
    nrow = kmt.shape[0]
    by_hi = _nt_dot(jnp.concatenate([km_hi, km_lo], axis=0), q_hi)
    gate_t = by_hi[:nrow] + by_hi[nrow:] + _nt_dot(km_hi, q_lo)

    blk = _iota((nb, qc), 0).astype(F32)
    past = _iota((nb, qc), 0) < qi
    bias_rows = []
    for h in range(A_HEADS):
        g = jnp.where(past, gate_t[h * nb:(h + 1) * nb, :], NEG_INF)
        sel = jnp.zeros((nb, qc), F32)
        for _ in range(MOBA_TOPK):
            top = jnp.max(g, axis=0, keepdims=True)
            first = jnp.min(jnp.where(g == top, blk, float(nb)), axis=0, keepdims=True)
            hit = jnp.logical_and(blk == first, top > NEG_INF)
            sel = jnp.where(hit, 1.0, sel)
            g = jnp.where(hit, NEG_INF, g)
        bias_rows.append(jnp.where(sel > 0.0, 0.0, NEG))
    if A_HEADS * nb < LANES:
        bias_rows.append(jnp.zeros((LANES - A_HEADS * nb, qc), F32))
    bias = jnp.concatenate(bias_rows, axis=0).T

    lane = _iota((qc, LANES), 1)
    klane = _iota((MOBA_BLOCK, LANES), 1)
    causal = _iota((qc, MOBA_BLOCK), 1) <= _iota((qc, MOBA_BLOCK), 0)
    own = pl.multiple_of(qi * MOBA_BLOCK, MOBA_BLOCK)

    for h in range(A_HEADS):
        ksl = slice((h // 2) * LANES, (h // 2 + 1) * LANES)
        qh = jnp.where((lane // A_HEAD_DIM) == (h % 2), qf[:, ksl], 0.0).astype(BF16)
        bh = jnp.where((lane // nb) == h, bias, 0.0).astype(BF16)
        qa_sc[h] = jnp.concatenate([qh, bh], axis=1)
        s = jnp.where(causal, _nt_dot(qh, k_ref[0, pl.ds(own, MOBA_BLOCK), ksl]), NEG)
        m0 = jnp.max(s, axis=1, keepdims=True)
        pr = jnp.exp2(s - m0)
        m_sc[h] = jnp.broadcast_to(m0, (qc, LANES))
        acc_sc[h] = jnp.dot(pr.astype(BF16), v_ref[0, pl.ds(own, MOBA_BLOCK), h * LANES:(h + 1) * LANES],
                            preferred_element_type=F32)

    def block(n):
        start = pl.multiple_of(n * MOBA_BLOCK, MOBA_BLOCK)
        onehot = jnp.where((klane % nb) == n, 1.0, 0.0).astype(BF16)
        for p in range(A_HEADS // 2):
            k_aug = jnp.concatenate([k_ref[0, pl.ds(start, MOBA_BLOCK), p * LANES:(p + 1) * LANES], onehot], axis=1)
            for h in (2 * p, 2 * p + 1):
                s = _nt_dot(qa_sc[h], k_aug)
                m_prev = m_sc[h]
                m_new = jnp.maximum(m_prev, jnp.max(s, axis=1, keepdims=True))
                alpha = jnp.exp2(m_prev - m_new)
                pr = jnp.exp2(s - jnp.concatenate([m_new, m_new], axis=1))
                pv = jnp.dot(pr.astype(BF16), v_ref[0, pl.ds(start, MOBA_BLOCK), h * LANES:(h + 1) * LANES],
                             preferred_element_type=F32)
                acc_sc[h] = alpha * acc_sc[h] + pv
                m_sc[h] = m_new

    def body(n4, carry):
        for j in range(4):
            block(4 * n4 + j)
        return carry

    lax.fori_loop(0, qi // 4, body, 0)
    rem = qi % 4
    done = qi - rem

    @pl.when(rem >= 2)
    def _():
        block(done)
        block(done + 1)

    @pl.when(rem % 2 == 1)
    def _():
        block(qi - 1)

    for p in range(A_HEADS // 2):
        a0 = acc_sc[2 * p]
        a1 = acc_sc[2 * p + 1]
        o0 = a0 / pltpu.roll(a0, A_HEAD_DIM, 1)
        o1 = a1 / pltpu.roll(a1, A_HEAD_DIM, 1)
        o_ref[0, :, p * LANES:(p + 1) * LANES] = jnp.where(
            lane < A_HEAD_DIM, o0, pltpu.roll(o1, A_HEAD_DIM, 1)).astype(BF16)


def _attn(q, k, v, km):
    b, s, _ = q.shape
    nb = s // MOBA_BLOCK
    assert nb * A_HEADS <= LANES, "block-bias columns must fit one lane group"
    qc = MOBA_BLOCK
    vmem = 2 * (s * A_WIDTH * 2 + s * 2 * A_WIDTH * 2 + qc * A_WIDTH * 6) + 16 * qc * 256 * 4 + (4 << 20)
    return pl.pallas_call(
        _attn_kernel, grid=(b, s // qc),
        in_specs=[pl.BlockSpec((1, qc, A_WIDTH), lambda bi, i: (bi, i, 0)),
                  pl.BlockSpec((1, s, A_WIDTH), lambda bi, i: (bi, 0, 0)),
                  pl.BlockSpec((1, s, 2 * A_WIDTH), lambda bi, i: (bi, 0, 0)),
                  pl.BlockSpec((1, nb, A_WIDTH), lambda bi, i: (bi, 0, 0))],
        out_specs=pl.BlockSpec((1, qc, A_WIDTH), lambda bi, i: (bi, i, 0)),
        out_shape=jax.ShapeDtypeStruct((b, s, A_WIDTH), BF16),
        scratch_shapes=[pltpu.VMEM((A_HEADS, qc, LANES), F32), pltpu.VMEM((A_HEADS, qc, LANES), F32),
                        pltpu.VMEM((A_HEADS, qc, 2 * LANES), BF16)],
        compiler_params=_params(vmem, 2), name="moba_attn",
    )(q, k, v, km)


def _mlstm_kernel(qk_ref, v_ref, o_ref, g_ref, cw_ref, cb_ref, hw_ref, y_ref, ext_sc, c_sc, m_sc):
    ci = pl.program_id(1)
    ln = qk_ref.shape[1]
    dh = M_HEAD_DIM

    @pl.when(ci == 0)
    def _():
        ext_sc[0:SUBLANES, :] = jnp.zeros((SUBLANES, 2 * M_WIDTH), F32)
        c_sc[...] = jnp.zeros(c_sc.shape, F32)
        m_sc[...] = jnp.zeros(m_sc.shape, F32)

    u = qk_ref[0]
    ext_sc[SUBLANES:SUBLANES + ln, :] = u
    conv = cb_ref[...]
    for j in range(M_CONV):
        conv = conv + cw_ref[j:j + 1, :] * ext_sc[pl.ds(SUBLANES - (M_CONV - 1) + j, ln), :]
    ext_sc[0:SUBLANES, :] = u[ln - SUBLANES:ln, :]
    act = conv * _sigmoid(conv)

    gates = g_ref[0]
    log_f = jnp.minimum(gates, 0.0) - jnp.log(1.0 + jnp.exp(-jnp.abs(gates)))
    row = _iota((ln, ln), 0)
    col = _iota((ln, ln), 1)
    causal = col <= row
    lf_hi = log_f.astype(BF16)
    lf_r = log_f - lf_hi.astype(F32)
    lf_mid = lf_r.astype(BF16)
    lf_lo = (lf_r - lf_mid.astype(F32)).astype(BF16)
    tri = causal.astype(BF16)
    two = jnp.dot(tri, jnp.concatenate([lf_hi, lf_mid], axis=1), preferred_element_type=F32)
    b_cols = two[:, :LANES] + two[:, LANES:] + jnp.dot(tri, lf_lo, preferred_element_type=F32)
    gates_t = gates.T
    b_rows = b_cols.T
    ones = jnp.ones((ln, dh), BF16)

    for h in range(M_HEADS):
        hs = slice(h * dh, (h + 1) * dh)
        qh = act[:, hs].astype(BF16)
        kh = act[:, M_WIDTH + h * dh:M_WIDTH + (h + 1) * dh] * (dh ** -0.5)
        b_col = b_cols[:, M_HEADS + h:M_HEADS + h + 1]
        b_row = b_rows[M_HEADS + h:M_HEADS + h + 1, :]
        i_col = gates[:, h:h + 1]
        i_row = gates_t[h:h + 1, :]
        b_last = b_col[ln - 1:ln, :]
        m_st = m_sc[h][0:1, 0:1]

        d = jnp.where(causal, b_col - b_row + i_row, NEG_INF)
        g = b_col + m_st
        m_t = jnp.maximum(g, jnp.max(d, axis=1, keepdims=True))
        w_intra = jnp.exp(d - m_t)
        w_inter = jnp.exp(g - m_t)
        qk = (_nt_dot(qh, kh.astype(BF16)) * w_intra).astype(BF16)
        v_aug = jnp.concatenate([v_ref[0, :, hs], ones], axis=1)
        c_aug = c_sc[h]
        res = (w_inter * jnp.dot(qh, c_aug.astype(BF16), preferred_element_type=F32)
               + jnp.dot(qk, v_aug, preferred_element_type=F32))
        num = res[:, :dh]
        den = res[:, dh:]
        h_t = num / jnp.maximum(jnp.abs(den), jnp.exp(-m_t))

        m_new = jnp.maximum(b_last + m_st, jnp.max(b_last - b_row + i_row, axis=1, keepdims=True))
        w_k = jnp.exp(b_last - b_col + i_col - m_new)
        decay = jnp.exp(b_last + m_st - m_new)
        kw_t = (kh * w_k).T.astype(BF16)
        c_sc[h] = decay * c_aug + jnp.dot(kw_t, v_aug, preferred_element_type=F32)
        m_sc[h] = jnp.broadcast_to(m_new, (SUBLANES, LANES))

        hn = h_t * lax.rsqrt(jnp.mean(h_t * h_t, axis=1, keepdims=True) + EPS) * hw_ref[:, hs]
        y_ref[0, :, hs] = (hn * _sigmoid(o_ref[0, :, hs])).astype(BF16)


def _mlstm(mqk, mv, mo, gif, conv_w, conv_b, head_w):
    b, s, _ = mqk.shape
    ln = MLSTM_CHUNK
    assert s % ln == 0
    blk = lambda w: pl.BlockSpec((1, ln, w), lambda bi, i: (bi, i, 0))
    full = lambda a: pl.BlockSpec(a.shape, lambda bi, i: (0,) * a.ndim)
    vmem = 2 * ln * (2 * M_WIDTH * 4 + M_WIDTH * 10 + LANES * 4) + 24 * ln * ln * 4 + (8 << 20)
    return pl.pallas_call(
        _mlstm_kernel, grid=(b, s // ln),
        in_specs=[blk(2 * M_WIDTH), blk(M_WIDTH), blk(M_WIDTH), blk(LANES),
                  full(conv_w), full(conv_b), full(head_w)],
        out_specs=blk(M_WIDTH),
        out_shape=jax.ShapeDtypeStruct((b, s, M_WIDTH), BF16),
        scratch_shapes=[pltpu.VMEM((ln + SUBLANES, 2 * M_WIDTH), F32),
                        pltpu.VMEM((M_HEADS, M_HEAD_DIM, 2 * M_HEAD_DIM), F32),
                        pltpu.VMEM((M_HEADS, SUBLANES, LANES), F32)],
        compiler_params=_params(vmem, 2), name="mlstm",
    )(mqk, mv, mo, gif, conv_w, conv_b, head_w)


def _merge_kernel(ya_ref, ym_ref, gam_ref, x_ref, wb_ref, wo_ref, nw_ref, wr_ref, br_ref,
                  h_ref, xn_ref, rmeta_ref, gmeta_ref, cnt_ref, cnt_sc):
    i = pl.program_id(0)
    tm = x_ref.shape[0]

    @pl.when(i == 0)
    def _():
        cnt_sc[...] = jnp.zeros(cnt_sc.shape, F32)

    pa = jnp.dot(ya_ref[...], wb_ref[0:A_WIDTH, :], preferred_element_type=F32)
    pm = jnp.dot(ym_ref[...], wb_ref[A_WIDTH:, :], preferred_element_type=F32)
    merged = _sigmoid(gam_ref[:, 0:D_MODEL]) * pa + _sigmoid(gam_ref[:, D_MODEL:]) * pm
    h = x_ref[...] + jnp.dot(merged.astype(BF16), wo_ref[...], preferred_element_type=F32)
    h_ref[...] = h
    xn = h * lax.rsqrt(jnp.mean(h * h, axis=-1, keepdims=True) + EPS) * nw_ref[...]
    _rows_to_tiles(xn_ref, xn)

    x_hi = xn.astype(BF16)
    x_lo = (xn - x_hi.astype(F32)).astype(BF16)
    both = jnp.dot(x_hi, wr_ref[...], preferred_element_type=F32)
    logits = (both[:, :LANES] + both[:, LANES:]
              + jnp.dot(x_lo, wr_ref[:, :LANES], preferred_element_type=F32) + br_ref[...])
    lane = _iota((tm, LANES), 1)
    lane_f = lane.astype(F32)
    cur = logits
    vals, idxs, hits = [], [], []
    for _ in range(TOP_K):
        top = jnp.max(cur, axis=1, keepdims=True)
        first = jnp.min(jnp.where(cur == top, lane_f, float(LANES)), axis=1, keepdims=True)
        hit = lane_f == first
        vals.append(top)
        idxs.append(first)
        hits.append(hit)
        cur = jnp.where(hit, NEG_INF, cur)
    exps = [jnp.exp(v - vals[0]) for v in vals]
    inv = 1.0 / (exps[0] + exps[1] + exps[2] + exps[3])

    onehot = jnp.zeros((tm, LANES), F32)
    for hit in hits:
        onehot = jnp.where(hit, 1.0, onehot)
    before = (_iota((tm, tm), 1) < _iota((tm, tm), 0)).astype(BF16)
    seen = jnp.dot(before, onehot.astype(BF16), preferred_element_type=F32) + cnt_sc[0:1, :]
    rmeta = jnp.zeros((tm, LANES), F32)
    gmeta = jnp.zeros((tm, LANES), F32)
    for kk in range(TOP_K):
        rank = jnp.sum(jnp.where(hits[kk], seen, 0.0), axis=1, keepdims=True)
        rmeta = jnp.where(lane == kk, idxs[kk], rmeta)
        rmeta = jnp.where(lane == TOP_K + kk, rank, rmeta)
        gmeta = jnp.where(lane == kk, exps[kk] * inv, gmeta)
    rmeta_ref[...] = rmeta.astype(I32)
    gmeta_ref[...] = gmeta
    cnt_sc[...] = cnt_sc[...] + jnp.sum(onehot, axis=0, keepdims=True)
    cnt_ref[...] = cnt_sc[...]


def _merge(ya, ym, gam, x2, wb, wo, nw, wr, br):
    t = x2.shape[0]
    tm = ROW_TILE
    row = lambda w: pl.BlockSpec((tm, w), lambda i: (i, 0))
    full = lambda a: pl.BlockSpec(a.shape, lambda i: (0,) * a.ndim)
    out_shapes = (jax.ShapeDtypeStruct((t, D_MODEL), F32), jax.ShapeDtypeStruct((t * SUBLANES, LANES), F32),
                  jax.ShapeDtypeStruct((t, LANES), I32), jax.ShapeDtypeStruct((t, LANES), F32),
                  jax.ShapeDtypeStruct((SUBLANES, LANES), F32))
    vmem = 2 * (wb.size * 2 + wo.size * 2 + wr.size * 2 + tm * D_MODEL * 22) + 16 * tm * D_MODEL * 4
    return pl.pallas_call(
        _merge_kernel, grid=(t // tm,),
        in_specs=[row(A_WIDTH), row(M_WIDTH), row(2 * D_MODEL), row(D_MODEL),
                  full(wb), full(wo), full(nw), full(wr), full(br)],
        out_specs=(row(D_MODEL), pl.BlockSpec((tm * SUBLANES, LANES), lambda i: (i, 0)), row(LANES), row(LANES),
                   pl.BlockSpec((SUBLANES, LANES), lambda i: (0, 0))),
        out_shape=out_shapes,
        scratch_shapes=[pltpu.VMEM((SUBLANES, LANES), F32)],
        compiler_params=_params(vmem), name="merge_route",
    )(ya, ym, gam, x2, wb, wo, nw, wr, br)


def _row_copy(src_ref, dst_ref, src_row, dst_row, n, sem):
    src = pl.multiple_of(src_row * SUBLANES, SUBLANES)
    dst = pl.multiple_of(dst_row * SUBLANES, SUBLANES)
    return pltpu.make_async_copy(src_ref.at[pl.ds(src, n * SUBLANES)], dst_ref.at[pl.ds(dst, n * SUBLANES)], sem)


def _push_kernel(tail_start_ref, tail_n_ref, n_used_ref, dest_ref, xn_ref, rows_ref, zero_sc, sem, zsem):
    i = pl.program_id(0)
    tm = xn_ref.shape[0] // SUBLANES

    def start(t, carry):
        for kk in range(TOP_K):
            _row_copy(xn_ref, rows_ref, t, dest_ref[t * TOP_K + kk], 1, sem).start(priority=kk % 2)
        return carry

    lax.fori_loop(0, tm, start, 0)

    @pl.when(i == 0)
    def _():
        zero_sc[...] = jnp.zeros(zero_sc.shape, F32)

        def fill(e, carry):
            base = tail_start_ref[e]

            def zstart(r, c):
                _row_copy(zero_sc, rows_ref, 0, base + r, 1, zsem).start()
                return c

            def zwait(r, c):
                _row_copy(zero_sc, rows_ref, 0, base + r, 1, zsem).wait()
                return c

            lax.fori_loop(0, tail_n_ref[e], zstart, 0)
            lax.fori_loop(0, tail_n_ref[e], zwait, 0)
            return carry

        lax.fori_loop(0, N_EXPERTS, fill, 0)

        def zblock(j, carry):
            cp = _row_copy(zero_sc, rows_ref, 0, j * MOE_BLOCK, MOE_BLOCK, zsem)
            cp.start()
            cp.wait()
            return carry

        lax.fori_loop(n_used_ref[0], rows_ref.shape[0] // (MOE_BLOCK * SUBLANES), zblock, 0)

    def wait(t, carry):
        for kk in range(TOP_K):
            _row_copy(xn_ref, rows_ref, t, dest_ref[t * TOP_K + kk], 1, sem).wait()
        return carry

    lax.fori_loop(0, tm, wait, 0)


def _push(xn2, dest, tail_start, tail_n, n_used, n_rows):
    t = xn2.shape[0] // SUBLANES
    tm = ROW_TILE
    grid_spec = pltpu.PrefetchScalarGridSpec(
        num_scalar_prefetch=3, grid=(t // tm,),
        in_specs=[pl.BlockSpec((tm * TOP_K,), lambda i, *_: (i,), memory_space=pltpu.SMEM),
                  pl.BlockSpec((tm * SUBLANES, LANES), lambda i, *_: (i, 0))],
        out_specs=pl.BlockSpec(memory_space=pl.ANY),
        scratch_shapes=[pltpu.VMEM((MOE_BLOCK * SUBLANES, LANES), F32),
                        pltpu.SemaphoreType.DMA(()), pltpu.SemaphoreType.DMA(())])
    return pl.pallas_call(
        _push_kernel, grid_spec=grid_spec,
        out_shape=jax.ShapeDtypeStruct((n_rows * SUBLANES, LANES), F32),
        compiler_params=_params(6 * tm * D_MODEL * 4 + (4 << 20)), name="moe_push",
    )(tail_start, tail_n, n_used, dest, xn2)


def _expert_kernel(blk_e_ref, n_used_ref, x_ref, w1_ref, b1_ref, w2_ref, b2_ref, y_ref, w1_sc, w2_sc):
    i = pl.program_id(0)
    prev = blk_e_ref[jnp.maximum(i - 1, 0)]
    fresh = jnp.logical_or(i == 0, blk_e_ref[i] != prev)

    @pl.when(jnp.logical_and(fresh, i < n_used_ref[0]))
    def _():
        w1_sc[...] = w1_ref[0].astype(BF16)
        w2_sc[...] = w2_ref[0].astype(BF16)

    @pl.when(i < n_used_ref[0])
    def _():
        xb = _tiles_to_rows(x_ref, MOE_BLOCK).astype(BF16)
        hdn = jnp.dot(xb, w1_sc[...], preferred_element_type=F32) + b1_ref[0]
        glu = jnp.minimum(hdn[:, :D_FF], SWIGLU_LIMIT)
        lin = jnp.clip(hdn[:, D_FF:], -SWIGLU_LIMIT, SWIGLU_LIMIT)
        act = glu * _sigmoid(SWIGLU_ALPHA * glu) * (lin + 1.0)
        _rows_to_tiles(y_ref, jnp.dot(act.astype(BF16), w2_sc[...], preferred_element_type=F32) + b2_ref[0])

    @pl.when(i >= n_used_ref[0])
    def _():
        y_ref[...] = jnp.zeros(y_ref.shape, F32)


def _expert(x_rows, blk_expert, n_used, w1, b1, w2, b2):
    n_rows = x_rows.shape[0] // SUBLANES
    n_blk = n_rows // MOE_BLOCK
    blk = lambda i, be, nu: (jnp.minimum(i, nu[0] - 1), 0)
    wsel = lambda i, be, nu: (be[i], 0, 0)
    grid_spec = pltpu.PrefetchScalarGridSpec(
        num_scalar_prefetch=2, grid=(n_blk,),
        in_specs=[pl.BlockSpec((MOE_BLOCK * SUBLANES, LANES), blk),
                  pl.BlockSpec((1, D_MODEL, 2 * D_FF), wsel),
                  pl.BlockSpec((1, 1, 2 * D_FF), wsel),
                  pl.BlockSpec((1, D_FF, D_MODEL), wsel),
                  pl.BlockSpec((1, 1, D_MODEL), wsel)],
        out_specs=pl.BlockSpec((MOE_BLOCK * SUBLANES, LANES), lambda i, be, nu: (i, 0)),
        scratch_shapes=[pltpu.VMEM((D_MODEL, 2 * D_FF), BF16), pltpu.VMEM((D_FF, D_MODEL), BF16)])
    vmem = 2 * (D_MODEL * 2 * D_FF * 4 + D_FF * D_MODEL * 4) + 3 * D_MODEL * D_FF * 2 + 12 * MOE_BLOCK * D_MODEL * 4
    return pl.pallas_call(
        _expert_kernel, grid_spec=grid_spec,
        out_shape=jax.ShapeDtypeStruct((n_rows * SUBLANES, LANES), F32),
        compiler_params=_params(vmem), name="moe_expert",
    )(blk_expert, n_used, x_rows, w1, b1, w2, b2)


def _final_kernel(dest_ref, dest_nxt_ref, h_ref, g_ref, nw_ref, rows_ref, o_ref, buf, sem):
    i = pl.program_id(0)
    last = pl.num_programs(0) - 1
    tm = h_ref.shape[0]

    def copies(idx_ref, slot, t):
        return [_row_copy(rows_ref, buf.at[slot, kk], idx_ref[t * TOP_K + kk], t, 1, sem.at[slot])
                for kk in range(TOP_K)]

    def start(idx_ref, slot):
        def body(t, carry):
            for kk, cp in enumerate(copies(idx_ref, slot, t)):
                cp.start(priority=kk % 2)
            return carry

        lax.fori_loop(0, tm, body, 0)

    def wait(idx_ref, slot):
        def body(t, carry):
            for cp in copies(idx_ref, slot, t):
                cp.wait()
            return carry

        lax.fori_loop(0, tm, body, 0)

    @pl.when(i == 0)
    def _():
        start(dest_ref, 0)

    def step(slot):
        @pl.when(i < last)
        def _():
            start(dest_nxt_ref, 1 - slot)

        wait(dest_ref, slot)
        gates = g_ref[...]
        y = h_ref[...]
        for kk in range(TOP_K):
            y = y + gates[:, kk:kk + 1] * _tiles_to_rows(buf.at[slot, kk], tm)
        o_ref[...] = y * lax.rsqrt(jnp.mean(y * y, axis=-1, keepdims=True) + EPS) * nw_ref[...]

    for slot in range(2):
        pl.when(i % 2 == slot)(functools.partial(step, slot))


def _final(h, gmeta, nw, y_rows, dest):
    t = h.shape[0]
    tm = ROW_TILE
    steps = t // tm
    return pl.pallas_call(
        _final_kernel, grid=(steps,),
        in_specs=[pl.BlockSpec((tm * TOP_K,), lambda i: (i,), memory_space=pltpu.SMEM),
                  pl.BlockSpec((tm * TOP_K,), lambda i: (jnp.minimum(i + 1, steps - 1),), memory_space=pltpu.SMEM),
                  pl.BlockSpec((tm, D_MODEL), lambda i: (i, 0)),
                  pl.BlockSpec((tm, LANES), lambda i: (i, 0)),
                  pl.BlockSpec(nw.shape, lambda i: (0, 0)),
                  pl.BlockSpec(memory_space=pl.ANY)],
        out_specs=pl.BlockSpec((tm, D_MODEL), lambda i: (i, 0)),
        out_shape=jax.ShapeDtypeStruct((t, D_MODEL), F32),
        scratch_shapes=[pltpu.VMEM((2, TOP_K, tm * SUBLANES, LANES), F32), pltpu.SemaphoreType.DMA((2,))],
        compiler_params=_params(18 * tm * D_MODEL * 4 + (4 << 20)), name="moe_combine",
    )(dest, dest, h, gmeta, nw, y_rows)


def _rope_tables(seq):
    inv = np.float32(ROPE_THETA) ** (-np.arange(0, A_HEAD_DIM, 2, dtype=np.float32) / np.float32(A_HEAD_DIM))
    ang = np.arange(seq, dtype=np.float32)[:, None] * inv[None, :].astype(np.float32)
    cos = np.cos(ang).astype(np.float32)
    sin = np.sin(ang).astype(np.float32)
    return (jnp.asarray(np.concatenate([cos] * 4, axis=-1)),
            jnp.asarray(np.concatenate([-sin, sin, -sin, sin], axis=-1)))


def _route_plan(rmeta, cnt, n_tok):
    idx = rmeta[:, 0:TOP_K]
    rank = rmeta[:, TOP_K:2 * TOP_K]
    counts = cnt[0, :N_EXPERTS].astype(I32)
    padded = (counts + MOE_BLOCK - 1) // MOE_BLOCK * MOE_BLOCK
    pad_end = jnp.cumsum(padded)
    pad_start = pad_end - padded
    dest = (pad_start[idx] + rank).reshape(n_tok * TOP_K).astype(I32)
    n_blk = (n_tok * TOP_K) // MOE_BLOCK + N_EXPERTS
    blk_row = jnp.arange(n_blk, dtype=I32) * MOE_BLOCK
    blk_expert = jnp.minimum(jnp.sum((pad_end[None, :] <= blk_row[:, None]).astype(I32), axis=1),
                             N_EXPERTS - 1).astype(I32)
    n_used = (pad_end[-1:] // MOE_BLOCK).astype(I32)
    return dest, blk_expert, n_used, (pad_start + counts).astype(I32), (padded - counts).astype(I32), n_blk


def _layer(h3, norm_mix_w, w_in, m_conv_w, m_conv_b, m_gate_bias, m_head_norm_w, w_branch, w_out,
           norm_ffn_w, w_router, b_router, w_mlp1, b_mlp1, w_mlp2, b_mlp2, norm_out_w):
    b, s, d = h3.shape
    t = b * s
    x2 = h3.reshape(t, d)
    c_if = 3 * A_WIDTH + 4 * M_WIDTH
    w_main = jnp.concatenate(
        [w_in[:, :c_if], w_in[:, c_if + 2 * M_HEADS:],
         jnp.pad(w_in[:, c_if:c_if + 2 * M_HEADS], ((0, 0), (0, LANES - 2 * M_HEADS)))], axis=1).astype(BF16)
    gate_bias = jnp.pad(m_gate_bias, (0, LANES - 2 * M_HEADS)).reshape(1, LANES)
    cos_t, sin_t = _rope_tables(s)

    aq, ak, av, km, mqk, mv, mo, gam, gif = _inproj(
        x2, norm_mix_w.reshape(1, d), w_main, gate_bias, cos_t, sin_t, s)

    ya = _attn(aq.reshape(b, s, A_WIDTH), ak.reshape(b, s, A_WIDTH), av.reshape(b, s, 2 * A_WIDTH),
               km.reshape(b, s // MOBA_BLOCK, A_WIDTH))
    ym = _mlstm(mqk.reshape(b, s, 2 * M_WIDTH), mv.reshape(b, s, M_WIDTH), mo.reshape(b, s, M_WIDTH),
                gif.reshape(b, s, LANES), m_conv_w, m_conv_b.reshape(1, -1), m_head_norm_w.reshape(1, -1))

    wr = jnp.pad(w_router, ((0, 0), (0, LANES - N_EXPERTS)))
    wr_hi = wr.astype(BF16)
    wr = jnp.concatenate([wr_hi, (wr - wr_hi.astype(F32)).astype(BF16)], axis=1)
    br = jnp.concatenate([b_router, jnp.full((LANES - N_EXPERTS,), NEG, F32)]).reshape(1, LANES)
    h, xn2, rmeta, gmeta, cnt = _merge(
        ya.reshape(t, A_WIDTH), ym.reshape(t, M_WIDTH), gam, x2, w_branch.astype(BF16), w_out.astype(BF16),
        norm_ffn_w.reshape(1, d), wr, br)

    dest, blk_expert, n_used, tail_start, tail_n, n_blk = _route_plan(rmeta, cnt, t)
    x_rows = _push(xn2, dest, tail_start, tail_n, n_used, n_blk * MOE_BLOCK)
    y_rows = _expert(x_rows, blk_expert, n_used, w_mlp1, b_mlp1.reshape(N_EXPERTS, 1, -1),
                     w_mlp2, b_mlp2.reshape(N_EXPERTS, 1, -1))
    out = _final(h, gmeta, norm_out_w.reshape(1, d), y_rows, dest)
    return out.reshape(b, s, d)


def kernel(x, norm_mix_w, w_in, m_conv_w, m_conv_b, m_gate_bias, m_head_norm_w, w_branch, w_out,
           norm_ffn_w, w_router, b_router, w_mlp1, b_mlp1, w_mlp2, b_mlp2, norm_final_w):
    depth = norm_mix_w.shape[0]
    assert depth == 1, "the final RMSNorm is fused into the layer's last kernel"
    return _layer(x, norm_mix_w[0], w_in[0], m_conv_w[0], m_conv_b[0], m_gate_bias[0], m_head_norm_w[0],
                  w_branch[0], w_out[0], norm_ffn_w[0], w_router[0], b_router[0], w_mlp1[0], b_mlp1[0],
                  w_mlp2[0], b_mlp2[0], norm_final_w)
```

```python
import functools

import jax
import jax.numpy as jnp
import numpy as np
from jax import lax
from jax.experimental import pallas as pl
from jax.experimental.pallas import tpu as pltpu

F32 = jnp.float32
BF16 = jnp.bfloat16
I32 = jnp.int32
HIGHEST = lax.Precision.HIGHEST

D_MODEL = 1024
A_HEADS = 8
A_HEAD_DIM = 64
A_WIDTH = A_HEADS * A_HEAD_DIM
MOBA_BLOCK = 256
MOBA_TOPK = 3
M_HEADS = 4
M_HEAD_DIM = 128
M_WIDTH = M_HEADS * M_HEAD_DIM
M_CONV = 4
N_EXPERTS = 32
TOP_K = 4
D_FF = 1024
SWIGLU_LIMIT = 7.0
SWIGLU_ALPHA = 1.702
MOE_BLOCK = 512
ROPE_THETA = 10000.0
EPS = 1e-6
NEG = -1e30
NEG_INF = float("-inf")

LANES = 128
SUBLANES = 8
VMEM_LIMIT_CAP = 56 * 1024 * 1024

C_AQ, C_AK, C_AV = 0, 512, 1024
C_MQK, C_MV, C_MO = 1536, 2560, 3072
C_GAM, C_GIF, C_END = 3584, 5632, 5760

ROW_TILE = 256
Q_SCALE = (A_HEAD_DIM ** -0.5) * 1.4426950408889634
MLSTM_CHUNK = 256


def _params(vmem_bytes, n_axes=1):
    return pltpu.CompilerParams(
        dimension_semantics=("arbitrary",) * n_axes,
        vmem_limit_bytes=int(min(max(vmem_bytes, 16 * 1024 * 1024), VMEM_LIMIT_CAP)))


def _iota(shape, dim):
    return lax.broadcasted_iota(I32, shape, dim)


def _sigmoid(x):
    return 1.0 / (1.0 + jnp.exp(-x))


def _rows_to_tiles(ref, value):
    n = value.shape[0]
    for j in range(SUBLANES):
        ref[pl.ds(j, n, stride=SUBLANES), :] = value[:, j * LANES:(j + 1) * LANES]


def _tiles_to_rows(ref, n):
    return jnp.concatenate([ref[pl.ds(j, n, stride=SUBLANES), :] for j in range(SUBLANES)], axis=1)


def _nt_dot(a, b, precision=None):
    return lax.dot_general(a, b, (((1,), (1,)), ((), ())), precision=precision,
                           preferred_element_type=F32)


def _inproj_kernel(x_ref, nw_ref, w_ref, gb_ref, cos_ref, sin_ref,
                   aq_ref, ak_ref, av_ref, km_ref, mqk_ref, mv_ref, mo_ref, gam_ref, gif_ref):
    tm = x_ref.shape[0]
    x = x_ref[...]
    xn = x * lax.rsqrt(jnp.mean(x * x, axis=-1, keepdims=True) + EPS) * nw_ref[...]
    xb = xn.astype(BF16)

    def mm(lo, hi):
        return jnp.dot(xb, w_ref[:, lo:hi], preferred_element_type=F32)

    cos = jnp.concatenate([cos_ref[...]] * 4, axis=1)
    sin = jnp.concatenate([sin_ref[...]] * 4, axis=1)
    lane = _iota((tm, A_WIDTH), 1)
    first_half = (lane & (A_HEAD_DIM - 1)) < (A_HEAD_DIM // 2)

    def rope(t):
        up = pltpu.roll(t, A_WIDTH - A_HEAD_DIM // 2, 1)
        dn = pltpu.roll(t, A_HEAD_DIM // 2, 1)
        return t * cos + jnp.where(first_half, up, dn) * sin

    q = rope(mm(C_AQ, C_AK)) * Q_SCALE
    k = rope(mm(C_AK, C_AV))
    aq_ref[...] = q
    ak_ref[...] = k.astype(BF16)
    km_ref[0] = jnp.mean(k, axis=0, keepdims=True)

    v = mm(C_AV, C_MQK)
    lane128 = _iota((tm, LANES), 1)
    low = lane128 < A_HEAD_DIM
    for p in range(A_HEADS // 2):
        vp = v[:, p * LANES:(p + 1) * LANES]
        av_ref[:, (2 * p) * LANES:(2 * p + 1) * LANES] = jnp.where(low, vp, 1.0).astype(BF16)
        av_ref[:, (2 * p + 1) * LANES:(2 * p + 2) * LANES] = jnp.where(
            low, pltpu.roll(vp, A_HEAD_DIM, 1), 1.0).astype(BF16)

    mqk_ref[...] = mm(C_MQK, C_MV)
    mv_ref[...] = mm(C_MV, C_MO).astype(BF16)
    mo_ref[...] = mm(C_MO, C_GAM)
    gam_ref[...] = mm(C_GAM, C_GIF)
    gif_ref[...] = mm(C_GIF, C_END) + gb_ref[...]


def _inproj(x2, nw, w_main, gate_bias, cos_t, sin_t, seq):
    t = x2.shape[0]
    tm = ROW_TILE
    assert seq % tm == 0 and tm == MOBA_BLOCK
    nsteps = t // tm
    spb = seq // tm
    row = lambda w: pl.BlockSpec((tm, w), lambda i: (i, 0))
    full = lambda a: pl.BlockSpec(a.shape, lambda i: (0,) * a.ndim)
    tab = pl.BlockSpec((tm, LANES), lambda i: (i % spb, 0))
    out_shapes = (
        jax.ShapeDtypeStruct((t, A_WIDTH), F32),
        jax.ShapeDtypeStruct((t, A_WIDTH), BF16),
        jax.ShapeDtypeStruct((t, 2 * A_WIDTH), BF16),
        jax.ShapeDtypeStruct((nsteps, 1, A_WIDTH), F32),
        jax.ShapeDtypeStruct((t, 2 * M_WIDTH), F32),
        jax.ShapeDtypeStruct((t, M_WIDTH), BF16),
        jax.ShapeDtypeStruct((t, M_WIDTH), F32),
        jax.ShapeDtypeStruct((t, 2 * D_MODEL), F32),
        jax.ShapeDtypeStruct((t, LANES), F32),
    )
    out_specs = (row(A_WIDTH), row(A_WIDTH), row(2 * A_WIDTH),
                 pl.BlockSpec((1, 1, A_WIDTH), lambda i: (i, 0, 0)),
                 row(2 * M_WIDTH), row(M_WIDTH), row(M_WIDTH), row(2 * D_MODEL), row(LANES))
    vmem = 2 * (w_main.size * 2 + tm * D_MODEL * 4 + tm * C_END * 4) + 8 * tm * C_END
    return pl.pallas_call(
        _inproj_kernel, grid=(nsteps,),
        in_specs=[row(D_MODEL), full(nw), full(w_main), full(gate_bias), tab, tab],
        out_specs=out_specs, out_shape=out_shapes,
        compiler_params=_params(vmem), name="inproj",
    )(x2, nw, w_main, gate_bias, cos_t, sin_t)


def _attn_kernel(q_ref, k_ref, v_ref, km_ref, o_ref, m_sc, acc_sc, qa_sc):
    qi = pl.program_id(1)
    qc = q_ref.shape[1]
    nb = km_ref.shape[1]
    qf = q_ref[0]
    km = km_ref[0]
    kmt = jnp.concatenate([km] * A_HEADS, axis=0)
    r = _iota(kmt.shape, 0)
    c = _iota(kmt.shape, 1)
    kmt = jnp.where((r // nb) == (c // A_HEAD_DIM), kmt, 0.0)
    km_hi = kmt.astype(BF16)
    km_lo = (kmt - km_hi.astype(F32)).astype(BF16)
    q_hi = qf.astype(BF16)
    q_lo = (qf - q_hi.astype(F32)).astype(BF16)
    nrow = kmt.shape[0]
    by_hi = _nt_dot(jnp.concatenate([km_hi, km_lo], axis=0), q_hi)
    gate_t = by_hi[:nrow] + by_hi[nrow:] + _nt_dot(km_hi, q_lo)

    blk = _iota((nb, qc), 0).astype(F32)
    past = _iota((nb, qc), 0) < qi
    bias_rows = []
    for h in range(A_HEADS):
        g = jnp.where(past, gate_t[h * nb:(h + 1) * nb, :], NEG_INF)
        sel = jnp.zeros((nb, qc), F32)
        for _ in range(MOBA_TOPK):
            top = jnp.max(g, axis=0, keepdims=True)
            first = jnp.min(jnp.where(g == top, blk, float(nb)), axis=0, keepdims=True)
            hit = jnp.logical_and(blk == first, top > NEG_INF)
            sel = jnp.where(hit, 1.0, sel)
            g = jnp.where(hit, NEG_INF, g)
        bias_rows.append(jnp.where(sel > 0.0, 0.0, NEG))
    if A_HEADS * nb < LANES:
        bias_rows.append(jnp.zeros((LANES - A_HEADS * nb, qc), F32))
    bias = jnp.concatenate(bias_rows, axis=0).T

    lane = _iota((qc, LANES), 1)
    klane = _iota((MOBA_BLOCK, LANES), 1)
    causal = _iota((qc, MOBA_BLOCK), 1) <= _iota((qc, MOBA_BLOCK), 0)
    own = pl.multiple_of(qi * MOBA_BLOCK, MOBA_BLOCK)

    for h in range(A_HEADS):
        ksl = slice((h // 2) * LANES, (h // 2 + 1) * LANES)
        qh = jnp.where((lane // A_HEAD_DIM) == (h % 2), qf[:, ksl], 0.0).astype(BF16)
        bh = jnp.where((lane // nb) == h, bias, 0.0).astype(BF16)
        qa_sc[h] = jnp.concatenate([qh, bh], axis=1)
        s = jnp.where(causal, _nt_dot(qh, k_ref[0, pl.ds(own, MOBA_BLOCK), ksl]), NEG)
        m0 = jnp.max(s, axis=1, keepdims=True)
        pr = jnp.exp2(s - m0)
        m_sc[h] = jnp.broadcast_to(m0, (qc, LANES))
        acc_sc[h] = jnp.dot(pr.astype(BF16), v_ref[0, pl.ds(own, MOBA_BLOCK), h * LANES:(h + 1) * LANES],
                            preferred_element_type=F32)

    def block(n):
        start = pl.multiple_of(n * MOBA_BLOCK, MOBA_BLOCK)
        onehot = jnp.where((klane % nb) == n, 1.0, 0.0).astype(BF16)
        for p in range(A_HEADS // 2):
            k_aug = jnp.concatenate([k_ref[0, pl.ds(start, MOBA_BLOCK), p * LANES:(p + 1) * LANES], onehot], axis=1)
            for h in (2 * p, 2 * p + 1):
                s = _nt_dot(qa_sc[h], k_aug)
                m_prev = m_sc[h]
                m_new = jnp.maximum(m_prev, jnp.max(s, axis=1, keepdims=True))
                alpha = jnp.exp2(m_prev - m_new)
                pr = jnp.exp2(s - jnp.concatenate([m_new, m_new], axis=1))
                pv = jnp.dot(pr.astype(BF16), v_ref[0, pl.ds(start, MOBA_BLOCK), h * LANES:(h + 1) * LANES],
                             preferred_element_type=F32)
                acc_sc[h] = alpha * acc_sc[h] + pv
                m_sc[h] = m_new

    def body(n4, carry):
        for j in range(4):
            block(4 * n4 + j)
        return carry

    lax.fori_loop(0, qi // 4, body, 0)
    rem = qi % 4
    done = qi - rem

    @pl.when(rem >= 2)
    def _():
        block(done)
        block(done + 1)

    @pl.when(rem % 2 == 1)
    def _():
        block(qi - 1)

    for p in range(A_HEADS // 2):
        a0 = acc_sc[2 * p]
        a1 = acc_sc[2 * p + 1]
        o0 = a0 / pltpu.roll(a0, A_HEAD_DIM, 1)
        o1 = a1 / pltpu.roll(a1, A_HEAD_DIM, 1)
        o_ref[0, :, p * LANES:(p + 1) * LANES] = jnp.where(
            lane < A_HEAD_DIM, o0, pltpu.roll(o1, A_HEAD_DIM, 1)).astype(BF16)


def _attn(q, k, v, km):
    b, s, _ = q.shape
    nb = s // MOBA_BLOCK
    assert nb * A_HEADS <= LANES, "block-bias columns must fit one lane group"
    qc = MOBA_BLOCK
    vmem = 2 * (s * A_WIDTH * 2 + s * 2 * A_WIDTH * 2 + qc * A_WIDTH * 6) + 16 * qc * 256 * 4 + (4 << 20)
    return pl.pallas_call(
        _attn_kernel, grid=(b, s // qc),
        in_specs=[pl.BlockSpec((1, qc, A_WIDTH), lambda bi, i: (bi, i, 0)),
                  pl.BlockSpec((1, s, A_WIDTH), lambda bi, i: (bi, 0, 0)),
                  pl.BlockSpec((1, s, 2 * A_WIDTH), lambda bi, i: (bi, 0, 0)),
                  pl.BlockSpec((1, nb, A_WIDTH), lambda bi, i: (bi, 0, 0))],
        out_specs=pl.BlockSpec((1, qc, A_WIDTH), lambda bi, i: (bi, i, 0)),
        out_shape=jax.ShapeDtypeStruct((b, s, A_WIDTH), BF16),
        scratch_shapes=[pltpu.VMEM((A_HEADS, qc, LANES), F32), pltpu.VMEM((A_HEADS, qc, LANES), F32),
                        pltpu.VMEM((A_HEADS, qc, 2 * LANES), BF16)],
        compiler_params=_params(vmem, 2), name="moba_attn",
    )(q, k, v, km)


def _mlstm_kernel(qk_ref, v_ref, o_ref, g_ref, cw_ref, cb_ref, hw_ref, y_ref, ext_sc, c_sc, m_sc):
    ci = pl.program_id(1)
    ln = qk_ref.shape[1]
    dh = M_HEAD_DIM

    @pl.when(ci == 0)
    def _():
        ext_sc[0:SUBLANES, :] = jnp.zeros((SUBLANES, 2 * M_WIDTH), F32)
        c_sc[...] = jnp.zeros(c_sc.shape, F32)
        m_sc[...] = jnp.zeros(m_sc.shape, F32)

    u = qk_ref[0]
    ext_sc[SUBLANES:SUBLANES + ln, :] = u
    conv = cb_ref[...]
    for j in range(M_CONV):
        conv = conv + cw_ref[j:j + 1, :] * ext_sc[pl.ds(SUBLANES - (M_CONV - 1) + j, ln), :]
    ext_sc[0:SUBLANES, :] = u[ln - SUBLANES:ln, :]
    act = conv * _sigmoid(conv)

    gates = g_ref[0]
    log_f = jnp.minimum(gates, 0.0) - jnp.log(1.0 + jnp.exp(-jnp.abs(gates)))
    row = _iota((ln, ln), 0)
    col = _iota((ln, ln), 1)
    causal = col <= row
    lf_hi = log_f.astype(BF16)
    lf_r = log_f - lf_hi.astype(F32)
    lf_mid = lf_r.astype(BF16)
    lf_lo = (lf_r - lf_mid.astype(F32)).astype(BF16)
    tri = causal.astype(BF16)
    two = jnp.dot(tri, jnp.concatenate([lf_hi, lf_mid], axis=1), preferred_element_type=F32)
    b_cols = two[:, :LANES] + two[:, LANES:] + jnp.dot(tri, lf_lo, preferred_element_type=F32)
    gates_t = gates.T
    b_rows = b_cols.T
    ones = jnp.ones((ln, dh), BF16)

    for h in range(M_HEADS):
        hs = slice(h * dh, (h + 1) * dh)
        qh = act[:, hs].astype(BF16)
        kh = act[:, M_WIDTH + h * dh:M_WIDTH + (h + 1) * dh] * (dh ** -0.5)
        b_col = b_cols[:, M_HEADS + h:M_HEADS + h + 1]
        b_row = b_rows[M_HEADS + h:M_HEADS + h + 1, :]
        i_col = gates[:, h:h + 1]
        i_row = gates_t[h:h + 1, :]
        b_last = b_col[ln - 1:ln, :]
        m_st = m_sc[h][0:1, 0:1]

        d = jnp.where(causal, b_col - b_row + i_row, NEG_INF)
        g = b_col + m_st
        m_t = jnp.maximum(g, jnp.max(d, axis=1, keepdims=True))
        w_intra = jnp.exp(d - m_t)
        w_inter = jnp.exp(g - m_t)
        qk = (_nt_dot(qh, kh.astype(BF16)) * w_intra).astype(BF16)
        v_aug = jnp.concatenate([v_ref[0, :, hs], ones], axis=1)
        c_aug = c_sc[h]
        res = (w_inter * jnp.dot(qh, c_aug.astype(BF16), preferred_element_type=F32)
               + jnp.dot(qk, v_aug, preferred_element_type=F32))
        num = res[:, :dh]
        den = res[:, dh:]
        h_t = num / jnp.maximum(jnp.abs(den), jnp.exp(-m_t))

        m_new = jnp.maximum(b_last + m_st, jnp.max(b_last - b_row + i_row, axis=1, keepdims=True))
        w_k = jnp.exp(b_last - b_col + i_col - m_new)
        decay = jnp.exp(b_last + m_st - m_new)
        kw_t = (kh * w_k).T.astype(BF16)
        c_sc[h] = decay * c_aug + jnp.dot(kw_t, v_aug, preferred_element_type=F32)
        m_sc[h] = jnp.broadcast_to(m_new, (SUBLANES, LANES))

        hn = h_t * lax.rsqrt(jnp.mean(h_t * h_t, axis=1, keepdims=True) + EPS) * hw_ref[:, hs]
        y_ref[0, :, hs] = (hn * _sigmoid(o_ref[0, :, hs])).astype(BF16)


def _mlstm(mqk, mv, mo, gif, conv_w, conv_b, head_w):
    b, s, _ = mqk.shape
    ln = MLSTM_CHUNK
    assert s % ln == 0
    blk = lambda w: pl.BlockSpec((1, ln, w), lambda bi, i: (bi, i, 0))
    full = lambda a: pl.BlockSpec(a.shape, lambda bi, i: (0,) * a.ndim)
    vmem = 2 * ln * (2 * M_WIDTH * 4 + M_WIDTH * 10 + LANES * 4) + 24 * ln * ln * 4 + (8 << 20)
    return pl.pallas_call(
        _mlstm_kernel, grid=(b, s // ln),
        in_specs=[blk(2 * M_WIDTH), blk(M_WIDTH), blk(M_WIDTH), blk(LANES),
                  full(conv_w), full(conv_b), full(head_w)],
        out_specs=blk(M_WIDTH),
        out_shape=jax.ShapeDtypeStruct((b, s, M_WIDTH), BF16),
        scratch_shapes=[pltpu.VMEM((ln + SUBLANES, 2 * M_WIDTH), F32),
                        pltpu.VMEM((M_HEADS, M_HEAD_DIM, 2 * M_HEAD_DIM), F32),
                        pltpu.VMEM((M_HEADS, SUBLANES, LANES), F32)],
        compiler_params=_params(vmem, 2), name="mlstm",
    )(mqk, mv, mo, gif, conv_w, conv_b, head_w)


def _merge_kernel(ya_ref, ym_ref, gam_ref, x_ref, wb_ref, wo_ref, nw_ref, wr_ref, br_ref,
                  h_ref, xn_ref, rmeta_ref, gmeta_ref, cnt_ref, cnt_sc):
    i = pl.program_id(0)
    tm = x_ref.shape[0]

    @pl.when(i == 0)
    def _():
        cnt_sc[...] = jnp.zeros(cnt_sc.shape, F32)

    pa = jnp.dot(ya_ref[...], wb_ref[0:A_WIDTH, :], preferred_element_type=F32)
    pm = jnp.dot(ym_ref[...], wb_ref[A_WIDTH:, :], preferred_element_type=F32)
    merged = _sigmoid(gam_ref[:, 0:D_MODEL]) * pa + _sigmoid(gam_ref[:, D_MODEL:]) * pm
    h = x_ref[...] + jnp.dot(merged.astype(BF16), wo_ref[...], preferred_element_type=F32)
    h_ref[...] = h
    xn = h * lax.rsqrt(jnp.mean(h * h, axis=-1, keepdims=True) + EPS) * nw_ref[...]
    _rows_to_tiles(xn_ref, xn)

    x_hi = xn.astype(BF16)
    x_lo = (xn - x_hi.astype(F32)).astype(BF16)
    both = jnp.dot(x_hi, wr_ref[...], preferred_element_type=F32)
    logits = (both[:, :LANES] + both[:, LANES:]
              + jnp.dot(x_lo, wr_ref[:, :LANES], preferred_element_type=F32) + br_ref[...])
    lane = _iota((tm, LANES), 1)
    lane_f = lane.astype(F32)
    cur = logits
    vals, idxs, hits = [], [], []
    for _ in range(TOP_K):
        top = jnp.max(cur, axis=1, keepdims=True)
        first = jnp.min(jnp.where(cur == top, lane_f, float(LANES)), axis=1, keepdims=True)
        hit = lane_f == first
        vals.append(top)
        idxs.append(first)
        hits.append(hit)
        cur = jnp.where(hit, NEG_INF, cur)
    exps = [jnp.exp(v - vals[0]) for v in vals]
    inv = 1.0 / (exps[0] + exps[1] + exps[2] + exps[3])

    onehot = jnp.zeros((tm, LANES), F32)
    for hit in hits:
        onehot = jnp.where(hit, 1.0, onehot)
    before = (_iota((tm, tm), 1) < _iota((tm, tm), 0)).astype(BF16)
    seen = jnp.dot(before, onehot.astype(BF16), preferred_element_type=F32) + cnt_sc[0:1, :]
    rmeta = jnp.zeros((tm, LANES), F32)
    gmeta = jnp.zeros((tm, LANES), F32)
    for kk in range(TOP_K):
        rank = jnp.sum(jnp.where(hits[kk], seen, 0.0), axis=1, keepdims=True)
        rmeta = jnp.where(lane == kk, idxs[kk], rmeta)
        rmeta = jnp.where(lane == TOP_K + kk, rank, rmeta)
        gmeta = jnp.where(lane == kk, exps[kk] * inv, gmeta)
    rmeta_ref[...] = rmeta.astype(I32)
    gmeta_ref[...] = gmeta
    cnt_sc[...] = cnt_sc[...] + jnp.sum(onehot, axis=0, keepdims=True)
    cnt_ref[...] = cnt_sc[...]


def _merge(ya, ym, gam, x2, wb, wo, nw, wr, br):
    t = x2.shape[0]
    tm = ROW_TILE
    row = lambda w: pl.BlockSpec((tm, w), lambda i: (i, 0))
    full = lambda a: pl.BlockSpec(a.shape, lambda i: (0,) * a.ndim)
    out_shapes = (jax.ShapeDtypeStruct((t, D_MODEL), F32), jax.ShapeDtypeStruct((t * SUBLANES, LANES), F32),
                  jax.ShapeDtypeStruct((t, LANES), I32), jax.ShapeDtypeStruct((t, LANES), F32),
                  jax.ShapeDtypeStruct((SUBLANES, LANES), F32))
    vmem = 2 * (wb.size * 2 + wo.size * 2 + wr.size * 2 + tm * D_MODEL * 22) + 16 * tm * D_MODEL * 4
    return pl.pallas_call(
        _merge_kernel, grid=(t // tm,),
        in_specs=[row(A_WIDTH), row(M_WIDTH), row(2 * D_MODEL), row(D_MODEL),
                  full(wb), full(wo), full(nw), full(wr), full(br)],
        out_specs=(row(D_MODEL), pl.BlockSpec((tm * SUBLANES, LANES), lambda i: (i, 0)), row(LANES), row(LANES),
                   pl.BlockSpec((SUBLANES, LANES), lambda i: (0, 0))),
        out_shape=out_shapes,
        scratch_shapes=[pltpu.VMEM((SUBLANES, LANES), F32)],
        compiler_params=_params(vmem), name="merge_route",
    )(ya, ym, gam, x2, wb, wo, nw, wr, br)


def _row_copy(src_ref, dst_ref, src_row, dst_row, n, sem):
    src = pl.multiple_of(src_row * SUBLANES, SUBLANES)
    dst = pl.multiple_of(dst_row * SUBLANES, SUBLANES)
    return pltpu.make_async_copy(src_ref.at[pl.ds(src, n * SUBLANES)], dst_ref.at[pl.ds(dst, n * SUBLANES)], sem)


def _push_kernel(tail_start_ref, tail_n_ref, n_used_ref, dest_ref, xn_ref, rows_ref, zero_sc, sem, zsem):
    i = pl.program_id(0)
    tm = xn_ref.shape[0] // SUBLANES

    def start(t, carry):
        for kk in range(TOP_K):
            _row_copy(xn_ref, rows_ref, t, dest_ref[t * TOP_K + kk], 1, sem).start(priority=kk % 2)
        return carry

    lax.fori_loop(0, tm, start, 0)

    @pl.when(i == 0)
    def _():
        zero_sc[...] = jnp.zeros(zero_sc.shape, F32)

        def fill(e, carry):
            base = tail_start_ref[e]

            def zstart(r, c):
                _row_copy(zero_sc, rows_ref, 0, base + r, 1, zsem).start()
                return c

            def zwait(r, c):
                _row_copy(zero_sc, rows_ref, 0, base + r, 1, zsem).wait()
                return c

            lax.fori_loop(0, tail_n_ref[e], zstart, 0)
            lax.fori_loop(0, tail_n_ref[e], zwait, 0)
            return carry

        lax.fori_loop(0, N_EXPERTS, fill, 0)

        def zblock(j, carry):
            cp = _row_copy(zero_sc, rows_ref, 0, j * MOE_BLOCK, MOE_BLOCK, zsem)
            cp.start()
            cp.wait()
            return carry

        lax.fori_loop(n_used_ref[0], rows_ref.shape[0] // (MOE_BLOCK * SUBLANES), zblock, 0)

    def wait(t, carry):
        for kk in range(TOP_K):
            _row_copy(xn_ref, rows_ref, t, dest_ref[t * TOP_K + kk], 1, sem).wait()
        return carry

    lax.fori_loop(0, tm, wait, 0)


def _push(xn2, dest, tail_start, tail_n, n_used, n_rows):
    t = xn2.shape[0] // SUBLANES
    tm = ROW_TILE
    grid_spec = pltpu.PrefetchScalarGridSpec(
        num_scalar_prefetch=3, grid=(t // tm,),
        in_specs=[pl.BlockSpec((tm * TOP_K,), lambda i, *_: (i,), memory_space=pltpu.SMEM),
                  pl.BlockSpec((tm * SUBLANES, LANES), lambda i, *_: (i, 0))],
        out_specs=pl.BlockSpec(memory_space=pl.ANY),
        scratch_shapes=[pltpu.VMEM((MOE_BLOCK * SUBLANES, LANES), F32),
                        pltpu.SemaphoreType.DMA(()), pltpu.SemaphoreType.DMA(())])
    return pl.pallas_call(
        _push_kernel, grid_spec=grid_spec,
        out_shape=jax.ShapeDtypeStruct((n_rows * SUBLANES, LANES), F32),
        compiler_params=_params(6 * tm * D_MODEL * 4 + (4 << 20)), name="moe_push",
    )(tail_start, tail_n, n_used, dest, xn2)


def _expert_kernel(blk_e_ref, n_used_ref, x_ref, w1_ref, b1_ref, w2_ref, b2_ref, y_ref, w1_sc, w2_sc):
    i = pl.program_id(0)
    prev = blk_e_ref[jnp.maximum(i - 1, 0)]
    fresh = jnp.logical_or(i == 0, blk_e_ref[i] != prev)

    @pl.when(jnp.logical_and(fresh, i < n_used_ref[0]))
    def _():
        w1_sc[...] = w1_ref[0].astype(BF16)
        w2_sc[...] = w2_ref[0].astype(BF16)

    @pl.when(i < n_used_ref[0])
    def _():
        xb = _tiles_to_rows(x_ref, MOE_BLOCK).astype(BF16)
        hdn = jnp.dot(xb, w1_sc[...], preferred_element_type=F32) + b1_ref[0]
        glu = jnp.minimum(hdn[:, :D_FF], SWIGLU_LIMIT)
        lin = jnp.clip(hdn[:, D_FF:], -SWIGLU_LIMIT, SWIGLU_LIMIT)
        act = glu * _sigmoid(SWIGLU_ALPHA * glu) * (lin + 1.0)
        _rows_to_tiles(y_ref, jnp.dot(act.astype(BF16), w2_sc[...], preferred_element_type=F32) + b2_ref[0])

    @pl.when(i >= n_used_ref[0])
    def _():
        y_ref[...] = jnp.zeros(y_ref.shape, F32)


def _expert(x_rows, blk_expert, n_used, w1, b1, w2, b2):
    n_rows = x_rows.shape[0] // SUBLANES
    n_blk = n_rows // MOE_BLOCK
    blk = lambda i, be, nu: (jnp.minimum(i, nu[0] - 1), 0)
    wsel = lambda i, be, nu: (be[i], 0, 0)
    grid_spec = pltpu.PrefetchScalarGridSpec(
        num_scalar_prefetch=2, grid=(n_blk,),
        in_specs=[pl.BlockSpec((MOE_BLOCK * SUBLANES, LANES), blk),
                  pl.BlockSpec((1, D_MODEL, 2 * D_FF), wsel),
                  pl.BlockSpec((1, 1, 2 * D_FF), wsel),
                  pl.BlockSpec((1, D_FF, D_MODEL), wsel),
                  pl.BlockSpec((1, 1, D_MODEL), wsel)],
        out_specs=pl.BlockSpec((MOE_BLOCK * SUBLANES, LANES), lambda i, be, nu: (i, 0)),
        scratch_shapes=[pltpu.VMEM((D_MODEL, 2 * D_FF), BF16), pltpu.VMEM((D_FF, D_MODEL), BF16)])
    vmem = 2 * (D_MODEL * 2 * D_FF * 4 + D_FF * D_MODEL * 4) + 3 * D_MODEL * D_FF * 2 + 12 * MOE_BLOCK * D_MODEL * 4
    return pl.pallas_call(
        _expert_kernel, grid_spec=grid_spec,
        out_shape=jax.ShapeDtypeStruct((n_rows * SUBLANES, LANES), F32),
        compiler_params=_params(vmem), name="moe_expert",
    )(blk_expert, n_used, x_rows, w1, b1, w2, b2)


def _final_kernel(dest_ref, dest_nxt_ref, h_ref, g_ref, nw_ref, rows_ref, o_ref, buf, sem):
    i = pl.program_id(0)
    last = pl.num_programs(0) - 1
    tm = h_ref.shape[0]

    def copies(idx_ref, slot, t):
        return [_row_copy(rows_ref, buf.at[slot, kk], idx_ref[t * TOP_K + kk], t, 1, sem.at[slot])
                for kk in range(TOP_K)]

    def start(idx_ref, slot):
        def body(t, carry):
            for kk, cp in enumerate(copies(idx_ref, slot, t)):
                cp.start(priority=kk % 2)
            return carry

        lax.fori_loop(0, tm, body, 0)

    def wait(idx_ref, slot):
        def body(t, carry):
            for cp in copies(idx_ref, slot, t):
                cp.wait()
            return carry

        lax.fori_loop(0, tm, body, 0)

    @pl.when(i == 0)
    def _():
        start(dest_ref, 0)

    def step(slot):
        @pl.when(i < last)
        def _():
            start(dest_nxt_ref, 1 - slot)

        wait(dest_ref, slot)
        gates = g_ref[...]
        y = h_ref[...]
        for kk in range(TOP_K):
            y = y + gates[:, kk:kk + 1] * _tiles_to_rows(buf.at[slot, kk], tm)
        o_ref[...] = y * lax.rsqrt(jnp.mean(y * y, axis=-1, keepdims=True) + EPS) * nw_ref[...]

    for slot in range(2):
        pl.when(i % 2 == slot)(functools.partial(step, slot))


def _final(h, gmeta, nw, y_rows, dest):
    t = h.shape[0]
    tm = ROW_TILE
    steps = t // tm
    return pl.pallas_call(
        _final_kernel, grid=(steps,),
        in_specs=[pl.BlockSpec((tm * TOP_K,), lambda i: (i,), memory_space=pltpu.SMEM),
                  pl.BlockSpec((tm * TOP_K,), lambda i: (jnp.minimum(i + 1, steps - 1),), memory_space=pltpu.SMEM),
                  pl.BlockSpec((tm, D_MODEL), lambda i: (i, 0)),
                  pl.BlockSpec((tm, LANES), lambda i: (i, 0)),
                  pl.BlockSpec(nw.shape, lambda i: (0, 0)),
                  pl.BlockSpec(memory_space=pl.ANY)],
        out_specs=pl.BlockSpec((tm, D_MODEL), lambda i: (i, 0)),
        out_shape=jax.ShapeDtypeStruct((t, D_MODEL), F32),
        scratch_shapes=[pltpu.VMEM((2, TOP_K, tm * SUBLANES, LANES), F32), pltpu.SemaphoreType.DMA((2,))],
        compiler_params=_params(18 * tm * D_MODEL * 4 + (4 << 20)), name="moe_combine",
    )(dest, dest, h, gmeta, nw, y_rows)


def _rope_tables(seq):
    inv = np.float32(ROPE_THETA) ** (-np.arange(0, A_HEAD_DIM, 2, dtype=np.float32) / np.float32(A_HEAD_DIM))
    ang = np.arange(seq, dtype=np.float32)[:, None] * inv[None, :].astype(np.float32)
    cos = np.cos(ang).astype(np.float32)
    sin = np.sin(ang).astype(np.float32)
    return (jnp.asarray(np.concatenate([cos] * 4, axis=-1)),
            jnp.asarray(np.concatenate([-sin, sin, -sin, sin], axis=-1)))


def _route_plan(rmeta, cnt, n_tok):
    idx = rmeta[:, 0:TOP_K]
    rank = rmeta[:, TOP_K:2 * TOP_K]
    counts = cnt[0, :N_EXPERTS].astype(I32)
    padded = (counts + MOE_BLOCK - 1) // MOE_BLOCK * MOE_BLOCK
    pad_end = jnp.cumsum(padded)
    pad_start = pad_end - padded
    dest = (pad_start[idx] + rank).reshape(n_tok * TOP_K).astype(I32)
    n_blk = (n_tok * TOP_K) // MOE_BLOCK + N_EXPERTS
    blk_row = jnp.arange(n_blk, dtype=I32) * MOE_BLOCK
    blk_expert = jnp.minimum(jnp.sum((pad_end[None, :] <= blk_row[:, None]).astype(I32), axis=1),
                             N_EXPERTS - 1).astype(I32)
    n_used = (pad_end[-1:] // MOE_BLOCK).astype(I32)
    return dest, blk_expert, n_used, (pad_start + counts).astype(I32), (padded - counts).astype(I32), n_blk


def _layer(h3, norm_mix_w, w_in, m_conv_w, m_conv_b, m_gate_bias, m_head_norm_w, w_branch, w_out,
           norm_ffn_w, w_router, b_router, w_mlp1, b_mlp1, w_mlp2, b_mlp2, norm_out_w):
    b, s, d = h3.shape
    t = b * s
    x2 = h3.reshape(t, d)
    c_if = 3 * A_WIDTH + 4 * M_WIDTH
    w_main = jnp.concatenate(
        [w_in[:, :c_if], w_in[:, c_if + 2 * M_HEADS:],
         jnp.pad(w_in[:, c_if:c_if + 2 * M_HEADS], ((0, 0), (0, LANES - 2 * M_HEADS)))], axis=1).astype(BF16)
    gate_bias = jnp.pad(m_gate_bias, (0, LANES - 2 * M_HEADS)).reshape(1, LANES)
    cos_t, sin_t = _rope_tables(s)

    aq, ak, av, km, mqk, mv, mo, gam, gif = _inproj(
        x2, norm_mix_w.reshape(1, d), w_main, gate_bias, cos_t, sin_t, s)

    ya = _attn(aq.reshape(b, s, A_WIDTH), ak.reshape(b, s, A_WIDTH), av.reshape(b, s, 2 * A_WIDTH),
               km.reshape(b, s // MOBA_BLOCK, A_WIDTH))
    ym = _mlstm(mqk.reshape(b, s, 2 * M_WIDTH), mv.reshape(b, s, M_WIDTH), mo.reshape(b, s, M_WIDTH),
                gif.reshape(b, s, LANES), m_conv_w, m_conv_b.reshape(1, -1), m_head_norm_w.reshape(1, -1))

    wr = jnp.pad(w_router, ((0, 0), (0, LANES - N_EXPERTS)))
    wr_hi = wr.astype(BF16)
    wr = jnp.concatenate([wr_hi, (wr - wr_hi.astype(F32)).astype(BF16)], axis=1)
    br = jnp.concatenate([b_router, jnp.full((LANES - N_EXPERTS,), NEG, F32)]).reshape(1, LANES)
    h, xn2, rmeta, gmeta, cnt = _merge(
        ya.reshape(t, A_WIDTH), ym.reshape(t, M_WIDTH), gam, x2, w_branch.astype(BF16), w_out.astype(BF16),
        norm_ffn_w.reshape(1, d), wr, br)

    dest, blk_expert, n_used, tail_start, tail_n, n_blk = _route_plan(rmeta, cnt, t)
    x_rows = _push(xn2, dest, tail_start, tail_n, n_used, n_blk * MOE_BLOCK)
    y_rows = _expert(x_rows, blk_expert, n_used, w_mlp1, b_mlp1.reshape(N_EXPERTS, 1, -1),
                     w_mlp2, b_mlp2.reshape(N_EXPERTS, 1, -1))
    out = _final(h, gmeta, norm_out_w.reshape(1, d), y_rows, dest)
    return out.reshape(b, s, d)


def kernel(x, norm_mix_w, w_in, m_conv_w, m_conv_b, m_gate_bias, m_head_norm_w, w_branch, w_out,
           norm_ffn_w, w_router, b_router, w_mlp1, b_mlp1, w_mlp2, b_mlp2, norm_final_w):
    depth = norm_mix_w.shape[0]
    assert depth == 1, "the final RMSNorm is fused into the layer's last kernel"
    return _layer(x, norm_mix_w[0], w_in[0], m_conv_w[0], m_conv_b[0], m_gate_bias[0], m_head_norm_w[0],
                  w_branch[0], w_out[0], norm_ffn_w[0], w_router[0], b_router[0], w_mlp1[0], b_mlp1[0],
                  w_mlp2[0], b_mlp2[0], norm_final_w)
```

```python
import functools

import jax
import jax.numpy as jnp
import numpy as np
from jax import lax
from jax.experimental import pallas as pl
from jax.experimental.pallas import tpu as pltpu

F32 = jnp.float32
BF16 = jnp.bfloat16
I32 = jnp.int32
HIGHEST = lax.Precision.HIGHEST

D_MODEL = 1024
A_HEADS = 8
A_HEAD_DIM = 64
A_WIDTH = A_HEADS * A_HEAD_DIM
MOBA_BLOCK = 256
MOBA_TOPK = 3
M_HEADS = 4
M_HEAD_DIM = 128
M_WIDTH = M_HEADS * M_HEAD_DIM
M_CONV = 4
N_EXPERTS = 32
TOP_K = 4
D_FF = 1024
SWIGLU_LIMIT = 7.0
SWIGLU_ALPHA = 1.702
MOE_BLOCK = 512
ROPE_THETA = 10000.0
EPS = 1e-6
NEG = -1e30
NEG_INF = float("-inf")

LANES = 128
SUBLANES = 8
VMEM_LIMIT_CAP = 56 * 1024 * 1024

C_AQ, C_AK, C_AV = 0, 512, 1024
C_MQK, C_MV, C_MO = 1536, 2560, 3072
C_GAM, C_GIF, C_END = 3584, 5632, 5760

ROW_TILE = 256
Q_SCALE = (A_HEAD_DIM ** -0.5) * 1.4426950408889634
MLSTM_CHUNK = 256


def _params(vmem_bytes, n_axes=1):
    return pltpu.CompilerParams(
        dimension_semantics=("arbitrary",) * n_axes,
        vmem_limit_bytes=int(min(max(vmem_bytes, 16 * 1024 * 1024), VMEM_LIMIT_CAP)))


def _iota(shape, dim):
    return lax.broadcasted_iota(I32, shape, dim)


def _sigmoid(x):
    return 1.0 / (1.0 + jnp.exp(-x))


def _rows_to_tiles(ref, value):
    n = value.shape[0]
    for j in range(SUBLANES):
        ref[pl.ds(j, n, stride=SUBLANES), :] = value[:, j * LANES:(j + 1) * LANES]


def _tiles_to_rows(ref, n):
    return jnp.concatenate([ref[pl.ds(j, n, stride=SUBLANES), :] for j in range(SUBLANES)], axis=1)


def _nt_dot(a, b, precision=None):
    return lax.dot_general(a, b, (((1,), (1,)), ((), ())), precision=precision,
                           preferred_element_type=F32)


def _inproj_kernel(x_ref, nw_ref, w_ref, gb_ref, cos_ref, sin_ref,
                   aq_ref, ak_ref, av_ref, km_ref, mqk_ref, mv_ref, mo_ref, gam_ref, gif_ref):
    tm = x_ref.shape[0]
    x = x_ref[...]
    xn = x * lax.rsqrt(jnp.mean(x * x, axis=-1, keepdims=True) + EPS) * nw_ref[...]
    xb = xn.astype(BF16)

    def mm(lo, hi):
        return jnp.dot(xb, w_ref[:, lo:hi], preferred_element_type=F32)

    cos = jnp.concatenate([cos_ref[...]] * 4, axis=1)
    sin = jnp.concatenate([sin_ref[...]] * 4, axis=1)
    lane = _iota((tm, A_WIDTH), 1)
    first_half = (lane & (A_HEAD_DIM - 1)) < (A_HEAD_DIM // 2)

    def rope(t):
        up = pltpu.roll(t, A_WIDTH - A_HEAD_DIM // 2, 1)
        dn = pltpu.roll(t, A_HEAD_DIM // 2, 1)
        return t * cos + jnp.where(first_half, up, dn) * sin

    q = rope(mm(C_AQ, C_AK)) * Q_SCALE
    k = rope(mm(C_AK, C_AV))
    aq_ref[...] = q
    ak_ref[...] = k.astype(BF16)
    km_ref[0] = jnp.mean(k, axis=0, keepdims=True)

    v = mm(C_AV, C_MQK)
    lane128 = _iota((tm, LANES), 1)
    low = lane128 < A_HEAD_DIM
    for p in range(A_HEADS // 2):
        vp = v[:, p * LANES:(p + 1) * LANES]
        av_ref[:, (2 * p) * LANES:(2 * p + 1) * LANES] = jnp.where(low, vp, 1.0).astype(BF16)
        av_ref[:, (2 * p + 1) * LANES:(2 * p + 2) * LANES] = jnp.where(
            low, pltpu.roll(vp, A_HEAD_DIM, 1), 1.0).astype(BF16)

    mqk_ref[...] = mm(C_MQK, C_MV)
    mv_ref[...] = mm(C_MV, C_MO).astype(BF16)
    mo_ref[...] = mm(C_MO, C_GAM)
    gam_ref[...] = mm(C_GAM, C_GIF)
    gif_ref[...] = mm(C_GIF, C_END) + gb_ref[...]


def _inproj(x2, nw, w_main, gate_bias, cos_t, sin_t, seq):
    t = x2.shape[0]
    tm = ROW_TILE
    assert seq % tm == 0 and tm == MOBA_BLOCK
    nsteps = t // tm
    spb = seq // tm
    row = lambda w: pl.BlockSpec((tm, w), lambda i: (i, 0))
    full = lambda a: pl.BlockSpec(a.shape, lambda i: (0,) * a.ndim)
    tab = pl.BlockSpec((tm, LANES), lambda i: (i % spb, 0))
    out_shapes = (
        jax.ShapeDtypeStruct((t, A_WIDTH), F32),
        jax.ShapeDtypeStruct((t, A_WIDTH), BF16),
        jax.ShapeDtypeStruct((t, 2 * A_WIDTH), BF16),
        jax.ShapeDtypeStruct((nsteps, 1, A_WIDTH), F32),
        jax.ShapeDtypeStruct((t, 2 * M_WIDTH), F32),
        jax.ShapeDtypeStruct((t, M_WIDTH), BF16),
        jax.ShapeDtypeStruct((t, M_WIDTH), F32),
        jax.ShapeDtypeStruct((t, 2 * D_MODEL), F32),
        jax.ShapeDtypeStruct((t, LANES), F32),
    )
    out_specs = (row(A_WIDTH), row(A_WIDTH), row(2 * A_WIDTH),
                 pl.BlockSpec((1, 1, A_WIDTH), lambda i: (i, 0, 0)),
                 row(2 * M_WIDTH), row(M_WIDTH), row(M_WIDTH), row(2 * D_MODEL), row(LANES))
    vmem = 2 * (w_main.size * 2 + tm * D_MODEL * 4 + tm * C_END * 4) + 8 * tm * C_END
    return pl.pallas_call(
        _inproj_kernel, grid=(nsteps,),
        in_specs=[row(D_MODEL), full(nw), full(w_main), full(gate_bias), tab, tab],
        out_specs=out_specs, out_shape=out_shapes,
        compiler_params=_params(vmem), name="inproj",
    )(x2, nw, w_main, gate_bias, cos_t, sin_t)


def _attn_kernel(q_ref, k_ref, v_ref, km_ref, o_ref, m_sc, acc_sc, qa_sc):
    qi = pl.program_id(1)
    qc = q_ref.shape[1]
    nb = km_ref.shape[1]
    qf = q_ref[0]
    km = km_ref[0]
    kmt = jnp.concatenate([km] * A_HEADS, axis=0)
    r = _iota(kmt.shape, 0)
    c = _iota(kmt.shape, 1)
    kmt = jnp.where((r // nb) == (c // A_HEAD_DIM), kmt, 0.0)
    km_hi = kmt.astype(BF16)
    km_lo = (kmt - km_hi.astype(F32)).astype(BF16)
    q_hi = qf.astype(BF16)
    q_lo = (qf - q_hi.astype(F32)).astype(BF16)
    nrow = kmt.shape[0]
    by_hi = _nt_dot(jnp.concatenate([km_hi, km_lo], axis=0), q_hi)
    gate_t = by_hi[:nrow] + by_hi[nrow:] + _nt_dot(km_hi, q_lo)

    blk = _iota((nb, qc), 0).astype(F32)
    past = _iota((nb, qc), 0) < qi
    bias_rows = []
    for h in range(A_HEADS):
        g = jnp.where(past, gate_t[h * nb:(h + 1) * nb, :], NEG_INF)
        sel = jnp.zeros((nb, qc), F32)
        for _ in range(MOBA_TOPK):
            top = jnp.max(g, axis=0, keepdims=True)
            first = jnp.min(jnp.where(g == top, blk, float(nb)), axis=0, keepdims=True)
            hit = jnp.logical_and(blk == first, top > NEG_INF)
            sel = jnp.where(hit, 1.0, sel)
            g = jnp.where(hit, NEG_INF, g)
        bias_rows.append(jnp.where(sel > 0.0, 0.0, NEG))
    if A_HEADS * nb < LANES:
        bias_rows.append(jnp.zeros((LANES - A_HEADS * nb, qc), F32))
    bias = jnp.concatenate(bias_rows, axis=0).T

    lane = _iota((qc, LANES), 1)
    klane = _iota((MOBA_BLOCK, LANES), 1)
    causal = _iota((qc, MOBA_BLOCK), 1) <= _iota((qc, MOBA_BLOCK), 0)
    own = pl.multiple_of(qi * MOBA_BLOCK, MOBA_BLOCK)

    for h in range(A_HEADS):
        ksl = slice((h // 2) * LANES, (h // 2 + 1) * LANES)
        qh = jnp.where((lane // A_HEAD_DIM) == (h % 2), qf[:, ksl], 0.0).astype(BF16)
        bh = jnp.where((lane // nb) == h, bias, 0.0).astype(BF16)
        qa_sc[h] = jnp.concatenate([qh, bh], axis=1)
        s = jnp.where(causal, _nt_dot(qh, k_ref[0, pl.ds(own, MOBA_BLOCK), ksl]), NEG)
        m0 = jnp.max(s, axis=1, keepdims=True)
        pr = jnp.exp2(s - m0)
        m_sc[h] = jnp.broadcast_to(m0, (qc, LANES))
        acc_sc[h] = jnp.dot(pr.astype(BF16), v_ref[0, pl.ds(own, MOBA_BLOCK), h * LANES:(h + 1) * LANES],
                            preferred_element_type=F32)

    def block(n):
        start = pl.multiple_of(n * MOBA_BLOCK, MOBA_BLOCK)
        onehot = jnp.where((klane % nb) == n, 1.0, 0.0).astype(BF16)
        for p in range(A_HEADS // 2):
            k_aug = jnp.concatenate([k_ref[0, pl.ds(start, MOBA_BLOCK), p * LANES:(p + 1) * LANES], onehot], axis=1)
            for h in (2 * p, 2 * p + 1):
                s = _nt_dot(qa_sc[h], k_aug)
                m_prev = m_sc[h]
                m_new = jnp.maximum(m_prev, jnp.max(s, axis=1, keepdims=True))
                alpha = jnp.exp2(m_prev - m_new)
                pr = jnp.exp2(s - jnp.concatenate([m_new, m_new], axis=1))
                pv = jnp.dot(pr.astype(BF16), v_ref[0, pl.ds(start, MOBA_BLOCK), h * LANES:(h + 1) * LANES],
                             preferred_element_type=F32)
                acc_sc[h] = alpha * acc_sc[h] + pv
                m_sc[h] = m_new

    def body(n4, carry):
        for j in range(4):
            block(4 * n4 + j)
        return carry

    lax.fori_loop(0, qi // 4, body, 0)
    rem = qi % 4
    done = qi - rem

    @pl.when(rem >= 2)
    def _():
        block(done)
        block(done + 1)

    @pl.when(rem % 2 == 1)
    def _():
        block(qi - 1)

    for p in range(A_HEADS // 2):
        a0 = acc_sc[2 * p]
        a1 = acc_sc[2 * p + 1]
        o0 = a0 / pltpu.roll(a0, A_HEAD_DIM, 1)
        o1 = a1 / pltpu.roll(a1, A_HEAD_DIM, 1)
        o_ref[0, :, p * LANES:(p + 1) * LANES] = jnp.where(
            lane < A_HEAD_DIM, o0, pltpu.roll(o1, A_HEAD_DIM, 1)).astype(BF16)


def _attn(q, k, v, km):
    b, s, _ = q.shape
    nb = s // MOBA_BLOCK
    assert nb * A_HEADS <= LANES, "block-bias columns must fit one lane group"
    qc = MOBA_BLOCK
    vmem = 2 * (s * A_WIDTH * 2 + s * 2 * A_WIDTH * 2 + qc * A_WIDTH * 6) + 16 * qc * 256 * 4 + (4 << 20)
    return pl.pallas_call(
        _attn_kernel, grid=(b, s // qc),
        in_specs=[pl.BlockSpec((1, qc, A_WIDTH), lambda bi, i: (bi, i, 0)),
                  pl.BlockSpec((1, s, A_WIDTH), lambda bi, i: (bi, 0, 0)),
                  pl.BlockSpec((1, s, 2 * A_WIDTH), lambda bi, i: (bi, 0, 0)),
                  pl.BlockSpec((1, nb, A_WIDTH), lambda bi, i: (bi, 0, 0))],
        out_specs=pl.BlockSpec((1, qc, A_WIDTH), lambda bi, i: (bi, i, 0)),
        out_shape=jax.ShapeDtypeStruct((b, s, A_WIDTH), BF16),
        scratch_shapes=[pltpu.VMEM((A_HEADS, qc, LANES), F32), pltpu.VMEM((A_HEADS, qc, LANES), F32),
                        pltpu.VMEM((A_HEADS, qc, 2 * LANES), BF16)],
        compiler_params=_params(vmem, 2), name="moba_attn",
    )(q, k, v, km)


def _mlstm_kernel(qk_ref, v_ref, o_ref, g_ref, cw_ref, cb_ref, hw_ref, y_ref, ext_sc, c_sc, m_sc):
    ci = pl.program_id(1)
    ln = qk_ref.shape[1]
    dh = M_HEAD_DIM

    @pl.when(ci == 0)
    def _():
        ext_sc[0:SUBLANES, :] = jnp.zeros((SUBLANES, 2 * M_WIDTH), F32)
        c_sc[...] = jnp.zeros(c_sc.shape, F32)
        m_sc[...] = jnp.zeros(m_sc.shape, F32)

    u = qk_ref[0]
    ext_sc[SUBLANES:SUBLANES + ln, :] = u
    conv = cb_ref[...]
    for j in range(M_CONV):
        conv = conv + cw_ref[j:j + 1, :] * ext_sc[pl.ds(SUBLANES - (M_CONV - 1) + j, ln), :]
    ext_sc[0:SUBLANES, :] = u[ln - SUBLANES:ln, :]
    act = conv * _sigmoid(conv)

    gates = g_ref[0]
    log_f = jnp.minimum(gates, 0.0) - jnp.log(1.0 + jnp.exp(-jnp.abs(gates)))
    row = _iota((ln, ln), 0)
    col = _iota((ln, ln), 1)
    causal = col <= row
    lf_hi = log_f.astype(BF16)
    lf_r = log_f - lf_hi.astype(F32)
    lf_mid = lf_r.astype(BF16)
    lf_lo = (lf_r - lf_mid.astype(F32)).astype(BF16)
    tri = causal.astype(BF16)
    two = jnp.dot(tri, jnp.concatenate([lf_hi, lf_mid], axis=1), preferred_element_type=F32)
    b_cols = two[:, :LANES] + two[:, LANES:] + jnp.dot(tri, lf_lo, preferred_element_type=F32)
    gates_t = gates.T
    b_rows = b_cols.T
    ones = jnp.ones((ln, dh), BF16)

    for h in range(M_HEADS):
        hs = slice(h * dh, (h + 1) * dh)
        qh = act[:, hs].astype(BF16)
        kh = act[:, M_WIDTH + h * dh:M_WIDTH + (h + 1) * dh] * (dh ** -0.5)
        b_col = b_cols[:, M_HEADS + h:M_HEADS + h + 1]
        b_row = b_rows[M_HEADS + h:M_HEADS + h + 1, :]
        i_col = gates[:, h:h + 1]
        i_row = gates_t[h:h + 1, :]
        b_last = b_col[ln - 1:ln, :]
        m_st = m_sc[h][0:1, 0:1]

        d = jnp.where(causal, b_col - b_row + i_row, NEG_INF)
        g = b_col + m_st
        m_t = jnp.maximum(g, jnp.max(d, axis=1, keepdims=True))
        w_intra = jnp.exp(d - m_t)
        w_inter = jnp.exp(g - m_t)
        qk = (_nt_dot(qh, kh.astype(BF16)) * w_intra).astype(BF16)
        v_aug = jnp.concatenate([v_ref[0, :, hs], ones], axis=1)
        c_aug = c_sc[h]
        res = (w_inter * jnp.dot(qh, c_aug.astype(BF16), preferred_element_type=F32)
               + jnp.dot(qk, v_aug, preferred_element_type=F32))
        num = res[:, :dh]
        den = res[:, dh:]
        h_t = num / jnp.maximum(jnp.abs(den), jnp.exp(-m_t))

        m_new = jnp.maximum(b_last + m_st, jnp.max(b_last - b_row + i_row, axis=1, keepdims=True))
        w_k = jnp.exp(b_last - b_col + i_col - m_new)
        decay = jnp.exp(b_last + m_st - m_new)
        kw_t = (kh * w_k).T.astype(BF16)
        c_sc[h] = decay * c_aug + jnp.dot(kw_t, v_aug, preferred_element_type=F32)
        m_sc[h] = jnp.broadcast_to(m_new, (SUBLANES, LANES))

        hn = h_t * lax.rsqrt(jnp.mean(h_t * h_t, axis=1, keepdims=True) + EPS) * hw_ref[:, hs]
        y_ref[0, :, hs] = (hn * _sigmoid(o_ref[0, :, hs])).astype(BF16)


def _mlstm(mqk, mv, mo, gif, conv_w, conv_b, head_w):
    b, s, _ = mqk.shape
    ln = MLSTM_CHUNK
    assert s % ln == 0
    blk = lambda w: pl.BlockSpec((1, ln, w), lambda bi, i: (bi, i, 0))
    full = lambda a: pl.BlockSpec(a.shape, lambda bi, i: (0,) * a.ndim)
    vmem = 2 * ln * (2 * M_WIDTH * 4 + M_WIDTH * 10 + LANES * 4) + 24 * ln * ln * 4 + (8 << 20)
    return pl.pallas_call(
        _mlstm_kernel, grid=(b, s // ln),
        in_specs=[blk(2 * M_WIDTH), blk(M_WIDTH), blk(M_WIDTH), blk(LANES),
                  full(conv_w), full(conv_b), full(head_w)],
        out_specs=blk(M_WIDTH),
        out_shape=jax.ShapeDtypeStruct((b, s, M_WIDTH), BF16),
        scratch_shapes=[pltpu.VMEM((ln + SUBLANES, 2 * M_WIDTH), F32),
                        pltpu.VMEM((M_HEADS, M_HEAD_DIM, 2 * M_HEAD_DIM), F32),
                        pltpu.VMEM((M_HEADS, SUBLANES, LANES), F32)],
        compiler_params=_params(vmem, 2), name="mlstm",
    )(mqk, mv, mo, gif, conv_w, conv_b, head_w)


def _merge_kernel(ya_ref, ym_ref, gam_ref, x_ref, wb_ref, wo_ref, nw_ref, wr_ref, br_ref,
                  h_ref, xn_ref, rmeta_ref, gmeta_ref, cnt_ref, cnt_sc):
    i = pl.program_id(0)
    tm = x_ref.shape[0]

    @pl.when(i == 0)
    def _():
        cnt_sc[...] = jnp.zeros(cnt_sc.shape, F32)

    pa = jnp.dot(ya_ref[...], wb_ref[0:A_WIDTH, :], preferred_element_type=F32)
    pm = jnp.dot(ym_ref[...], wb_ref[A_WIDTH:, :], preferred_element_type=F32)
    merged = _sigmoid(gam_ref[:, 0:D_MODEL]) * pa + _sigmoid(gam_ref[:, D_MODEL:]) * pm
    h = x_ref[...] + jnp.dot(merged.astype(BF16), wo_ref[...], preferred_element_type=F32)
    h_ref[...] = h
    xn = h * lax.rsqrt(jnp.mean(h * h, axis=-1, keepdims=True) + EPS) * nw_ref[...]
    _rows_to_tiles(xn_ref, xn)

    x_hi = xn.astype(BF16)
    x_lo = (xn - x_hi.astype(F32)).astype(BF16)
    both = jnp.dot(x_hi, wr_ref[...], preferred_element_type=F32)
    logits = (both[:, :LANES] + both[:, LANES:]
              + jnp.dot(x_lo, wr_ref[:, :LANES], preferred_element_type=F32) + br_ref[...])
    lane = _iota((tm, LANES), 1)
    lane_f = lane.astype(F32)
    cur = logits
    vals, idxs, hits = [], [], []
    for _ in range(TOP_K):
        top = jnp.max(cur, axis=1, keepdims=True)
        first = jnp.min(jnp.where(cur == top, lane_f, float(LANES)), axis=1, keepdims=True)
        hit = lane_f == first
        vals.append(top)
        idxs.append(first)
        hits.append(hit)
        cur = jnp.where(hit, NEG_INF, cur)
    exps = [jnp.exp(v - vals[0]) for v in vals]
    inv = 1.0 / (exps[0] + exps[1] + exps[2] + exps[3])

    onehot = jnp.zeros((tm, LANES), F32)
    for hit in hits:
        onehot = jnp.where(hit, 1.0, onehot)
    before = (_iota((tm, tm), 1) < _iota((tm, tm), 0)).astype(BF16)
    seen = jnp.dot(before, onehot.astype(BF16), preferred_element_type=F32) + cnt_sc[0:1, :]
    rmeta = jnp.zeros((tm, LANES), F32)
    gmeta = jnp.zeros((tm, LANES), F32)
    for kk in range(TOP_K):
        rank = jnp.sum(jnp.where(hits[kk], seen, 0.0), axis=1, keepdims=True)
        rmeta = jnp.where(lane == kk, idxs[kk], rmeta)
        rmeta = jnp.where(lane == TOP_K + kk, rank, rmeta)
        gmeta = jnp.where(lane == kk, exps[kk] * inv, gmeta)
    rmeta_ref[...] = rmeta.astype(I32)
    gmeta_ref[...] = gmeta
    cnt_sc[...] = cnt_sc[...] + jnp.sum(onehot, axis=0, keepdims=True)
    cnt_ref[...] = cnt_sc[...]


def _merge(ya, ym, gam, x2, wb, wo, nw, wr, br):
    t = x2.shape[0]
    tm = ROW_TILE
    row = lambda w: pl.BlockSpec((tm, w), lambda i: (i, 0))
    full = lambda a: pl.BlockSpec(a.shape, lambda i: (0,) * a.ndim)
    out_shapes = (jax.ShapeDtypeStruct((t, D_MODEL), F32), jax.ShapeDtypeStruct((t * SUBLANES, LANES), F32),
                  jax.ShapeDtypeStruct((t, LANES), I32), jax.ShapeDtypeStruct((t, LANES), F32),
                  jax.ShapeDtypeStruct((SUBLANES, LANES), F32))
    vmem = 2 * (wb.size * 2 + wo.size * 2 + wr.size * 2 + tm * D_MODEL * 22) + 16 * tm * D_MODEL * 4
    return pl.pallas_call(
        _merge_kernel, grid=(t // tm,),
        in_specs=[row(A_WIDTH), row(M_WIDTH), row(2 * D_MODEL), row(D_MODEL),
                  full(wb), full(wo), full(nw), full(wr), full(br)],
        out_specs=(row(D_MODEL), pl.BlockSpec((tm * SUBLANES, LANES), lambda i: (i, 0)), row(LANES), row(LANES),
                   pl.BlockSpec((SUBLANES, LANES), lambda i: (0, 0))),
        out_shape=out_shapes,
        scratch_shapes=[pltpu.VMEM((SUBLANES, LANES), F32)],
        compiler_params=_params(vmem), name="merge_route",
    )(ya, ym, gam, x2, wb, wo, nw, wr, br)


def _row_copy(src_ref, dst_ref, src_row, dst_row, n, sem):
    src = pl.multiple_of(src_row * SUBLANES, SUBLANES)
    dst = pl.multiple_of(dst_row * SUBLANES, SUBLANES)
    return pltpu.make_async_copy(src_ref.at[pl.ds(src, n * SUBLANES)], dst_ref.at[pl.ds(dst, n * SUBLANES)], sem)


def _push_kernel(tail_start_ref, tail_n_ref, n_used_ref, dest_ref, xn_ref, rows_ref, zero_sc, sem, zsem):
    i = pl.program_id(0)
    tm = xn_ref.shape[0] // SUBLANES

    def start(t, carry):
        for kk in range(TOP_K):
            _row_copy(xn_ref, rows_ref, t, dest_ref[t * TOP_K + kk], 1, sem).start(priority=kk % 2)
        return carry

    lax.fori_loop(0, tm, start, 0)

    @pl.when(i == 0)
    def _():
        zero_sc[...] = jnp.zeros(zero_sc.shape, F32)

        def fill(e, carry):
            base = tail_start_ref[e]

            def zstart(r, c):
                _row_copy(zero_sc, rows_ref, 0, base + r, 1, zsem).start()
                return c

            lax.fori_loop(0, tail_n_ref[e], zstart, 0)
            return carry

        def drain(e, carry):
            base = tail_start_ref[e]

            def zwait(r, c):
                _row_copy(zero_sc, rows_ref, 0, base + r, 1, zsem).wait()
                return c

            lax.fori_loop(0, tail_n_ref[e], zwait, 0)
            return carry

        lax.fori_loop(0, N_EXPERTS, fill, 0)
        lax.fori_loop(0, N_EXPERTS, drain, 0)

        def zblock(j, carry):
            cp = _row_copy(zero_sc, rows_ref, 0, j * MOE_BLOCK, MOE_BLOCK, zsem)
            cp.start()
            cp.wait()
            return carry

        lax.fori_loop(n_used_ref[0], rows_ref.shape[0] // (MOE_BLOCK * SUBLANES), zblock, 0)

    def wait(t, carry):
        for kk in range(TOP_K):
            _row_copy(xn_ref, rows_ref, t, dest_ref[t * TOP_K + kk], 1, sem).wait()
        return carry

    lax.fori_loop(0, tm, wait, 0)


def _push(xn2, dest, tail_start, tail_n, n_used, n_rows):
    t = xn2.shape[0] // SUBLANES
    tm = ROW_TILE
    grid_spec = pltpu.PrefetchScalarGridSpec(
        num_scalar_prefetch=3, grid=(t // tm,),
        in_specs=[pl.BlockSpec((tm * TOP_K,), lambda i, *_: (i,), memory_space=pltpu.SMEM),
                  pl.BlockSpec((tm * SUBLANES, LANES), lambda i, *_: (i, 0))],
        out_specs=pl.BlockSpec(memory_space=pl.ANY),
        scratch_shapes=[pltpu.VMEM((MOE_BLOCK * SUBLANES, LANES), F32),
                        pltpu.SemaphoreType.DMA(()), pltpu.SemaphoreType.DMA(())])
    return pl.pallas_call(
        _push_kernel, grid_spec=grid_spec,
        out_shape=jax.ShapeDtypeStruct((n_rows * SUBLANES, LANES), F32),
        compiler_params=_params(6 * tm * D_MODEL * 4 + (4 << 20)), name="moe_push",
    )(tail_start, tail_n, n_used, dest, xn2)


def _expert_kernel(blk_e_ref, n_used_ref, x_ref, w1_ref, b1_ref, w2_ref, b2_ref, y_ref, w1_sc, w2_sc):
    i = pl.program_id(0)
    prev = blk_e_ref[jnp.maximum(i - 1, 0)]
    fresh = jnp.logical_or(i == 0, blk_e_ref[i] != prev)

    @pl.when(jnp.logical_and(fresh, i < n_used_ref[0]))
    def _():
        w1_sc[...] = w1_ref[0].astype(BF16)
        w2_sc[...] = w2_ref[0].astype(BF16)

    @pl.when(i < n_used_ref[0])
    def _():
        xb = _tiles_to_rows(x_ref, MOE_BLOCK).astype(BF16)
        hdn = jnp.dot(xb, w1_sc[...], preferred_element_type=F32) + b1_ref[0]
        glu = jnp.minimum(hdn[:, :D_FF], SWIGLU_LIMIT)
        lin = jnp.clip(hdn[:, D_FF:], -SWIGLU_LIMIT, SWIGLU_LIMIT)
        act = glu * _sigmoid(SWIGLU_ALPHA * glu) * (lin + 1.0)
        _rows_to_tiles(y_ref, jnp.dot(act.astype(BF16), w2_sc[...], preferred_element_type=F32) + b2_ref[0])

    @pl.when(i >= n_used_ref[0])
    def _():
        y_ref[...] = jnp.zeros(y_ref.shape, F32)


def _expert(x_rows, blk_expert, n_used, w1, b1, w2, b2):
    n_rows = x_rows.shape[0] // SUBLANES
    n_blk = n_rows // MOE_BLOCK
    blk = lambda i, be, nu: (jnp.minimum(i, nu[0] - 1), 0)
    wsel = lambda i, be, nu: (be[i], 0, 0)
    grid_spec = pltpu.PrefetchScalarGridSpec(
        num_scalar_prefetch=2, grid=(n_blk,),
        in_specs=[pl.BlockSpec((MOE_BLOCK * SUBLANES, LANES), blk),
                  pl.BlockSpec((1, D_MODEL, 2 * D_FF), wsel),
                  pl.BlockSpec((1, 1, 2 * D_FF), wsel),
                  pl.BlockSpec((1, D_FF, D_MODEL), wsel),
                  pl.BlockSpec((1, 1, D_MODEL), wsel)],
        out_specs=pl.BlockSpec((MOE_BLOCK * SUBLANES, LANES), lambda i, be, nu: (i, 0)),
        scratch_shapes=[pltpu.VMEM((D_MODEL, 2 * D_FF), BF16), pltpu.VMEM((D_FF, D_MODEL), BF16)])
    vmem = 2 * (D_MODEL * 2 * D_FF * 4 + D_FF * D_MODEL * 4) + 3 * D_MODEL * D_FF * 2 + 12 * MOE_BLOCK * D_MODEL * 4
    return pl.pallas_call(
        _expert_kernel, grid_spec=grid_spec,
        out_shape=jax.ShapeDtypeStruct((n_rows * SUBLANES, LANES), F32),
        compiler_params=_params(vmem), name="moe_expert",
    )(blk_expert, n_used, x_rows, w1, b1, w2, b2)


def _final_kernel(dest_ref, dest_nxt_ref, h_ref, g_ref, nw_ref, rows_ref, o_ref, buf, sem):
    i = pl.program_id(0)
    last = pl.num_programs(0) - 1
    tm = h_ref.shape[0]

    def copies(idx_ref, slot, t):
        return [_row_copy(rows_ref, buf.at[slot, kk], idx_ref[t * TOP_K + kk], t, 1, sem.at[slot])
                for kk in range(TOP_K)]

    def start(idx_ref, slot):
        def body(t, carry):
            for kk, cp in enumerate(copies(idx_ref, slot, t)):
                cp.start(priority=kk % 2)
            return carry

        lax.fori_loop(0, tm, body, 0)

    def wait(idx_ref, slot):
        def body(t, carry):
            for cp in copies(idx_ref, slot, t):
                cp.wait()
            return carry

        lax.fori_loop(0, tm, body, 0)

    @pl.when(i == 0)
    def _():
        start(dest_ref, 0)

    def step(slot):
        @pl.when(i < last)
        def _():
            start(dest_nxt_ref, 1 - slot)

        wait(dest_ref, slot)
        gates = g_ref[...]
        y = h_ref[...]
        for kk in range(TOP_K):
            y = y + gates[:, kk:kk + 1] * _tiles_to_rows(buf.at[slot, kk], tm)
        o_ref[...] = y * lax.rsqrt(jnp.mean(y * y, axis=-1, keepdims=True) + EPS) * nw_ref[...]

    for slot in range(2):
        pl.when(i % 2 == slot)(functools.partial(step, slot))


def _final(h, gmeta, nw, y_rows, dest):
    t = h.shape[0]
    tm = ROW_TILE
    steps = t // tm
    return pl.pallas_call(
        _final_kernel, grid=(steps,),
        in_specs=[pl.BlockSpec((tm * TOP_K,), lambda i: (i,), memory_space=pltpu.SMEM),
                  pl.BlockSpec((tm * TOP_K,), lambda i: (jnp.minimum(i + 1, steps - 1),), memory_space=pltpu.SMEM),
                  pl.BlockSpec((tm, D_MODEL), lambda i: (i, 0)),
                  pl.BlockSpec((tm, LANES), lambda i: (i, 0)),
                  pl.BlockSpec(nw.shape, lambda i: (0, 0)),
                  pl.BlockSpec(memory_space=pl.ANY)],
        out_specs=pl.BlockSpec((tm, D_MODEL), lambda i: (i, 0)),
        out_shape=jax.ShapeDtypeStruct((t, D_MODEL), F32),
        scratch_shapes=[pltpu.VMEM((2, TOP_K, tm * SUBLANES, LANES), F32), pltpu.SemaphoreType.DMA((2,))],
        compiler_params=_params(18 * tm * D_MODEL * 4 + (4 << 20)), name="moe_combine",
    )(dest, dest, h, gmeta, nw, y_rows)


def _rope_tables(seq):
    inv = np.float32(ROPE_THETA) ** (-np.arange(0, A_HEAD_DIM, 2, dtype=np.float32) / np.float32(A_HEAD_DIM))
    ang = np.arange(seq, dtype=np.float32)[:, None] * inv[None, :].astype(np.float32)
    cos = np.cos(ang).astype(np.float32)
    sin = np.sin(ang).astype(np.float32)
    return (jnp.asarray(np.concatenate([cos] * 4, axis=-1)),
            jnp.asarray(np.concatenate([-sin, sin, -sin, sin], axis=-1)))


def _route_plan(rmeta, cnt, n_tok):
    idx = rmeta[:, 0:TOP_K]
    rank = rmeta[:, TOP_K:2 * TOP_K]
    counts = cnt[0, :N_EXPERTS].astype(I32)
    padded = (counts + MOE_BLOCK - 1) // MOE_BLOCK * MOE_BLOCK
    pad_end = jnp.cumsum(padded)
    pad_start = pad_end - padded
    dest = (pad_start[idx] + rank).reshape(n_tok * TOP_K).astype(I32)
    n_blk = (n_tok * TOP_K) // MOE_BLOCK + N_EXPERTS
    blk_row = jnp.arange(n_blk, dtype=I32) * MOE_BLOCK
    blk_expert = jnp.minimum(jnp.sum((pad_end[None, :] <= blk_row[:, None]).astype(I32), axis=1),
                             N_EXPERTS - 1).astype(I32)
    n_used = (pad_end[-1:] // MOE_BLOCK).astype(I32)
    return dest, blk_expert, n_used, (pad_start + counts).astype(I32), (padded - counts).astype(I32), n_blk


def _layer(h3, norm_mix_w, w_in, m_conv_w, m_conv_b, m_gate_bias, m_head_norm_w, w_branch, w_out,
           norm_ffn_w, w_router, b_router, w_mlp1, b_mlp1, w_mlp2, b_mlp2, norm_out_w):
    b, s, d = h3.shape
    t = b * s
    x2 = h3.reshape(t, d)
    c_if = 3 * A_WIDTH + 4 * M_WIDTH
    w_main = jnp.concatenate(
        [w_in[:, :c_if], w_in[:, c_if + 2 * M_HEADS:],
         jnp.pad(w_in[:, c_if:c_if + 2 * M_HEADS], ((0, 0), (0, LANES - 2 * M_HEADS)))], axis=1).astype(BF16)
    gate_bias = jnp.pad(m_gate_bias, (0, LANES - 2 * M_HEADS)).reshape(1, LANES)
    cos_t, sin_t = _rope_tables(s)

    aq, ak, av, km, mqk, mv, mo, gam, gif = _inproj(
        x2, norm_mix_w.reshape(1, d), w_main, gate_bias, cos_t, sin_t, s)

    ya = _attn(aq.reshape(b, s, A_WIDTH), ak.reshape(b, s, A_WIDTH), av.reshape(b, s, 2 * A_WIDTH),
               km.reshape(b, s // MOBA_BLOCK, A_WIDTH))
    ym = _mlstm(mqk.reshape(b, s, 2 * M_WIDTH), mv.reshape(b, s, M_WIDTH), mo.reshape(b, s, M_WIDTH),
                gif.reshape(b, s, LANES), m_conv_w, m_conv_b.reshape(1, -1), m_head_norm_w.reshape(1, -1))

    wr = jnp.pad(w_router, ((0, 0), (0, LANES - N_EXPERTS)))
    wr_hi = wr.astype(BF16)
    wr = jnp.concatenate([wr_hi, (wr - wr_hi.astype(F32)).astype(BF16)], axis=1)
    br = jnp.concatenate([b_router, jnp.full((LANES - N_EXPERTS,), NEG, F32)]).reshape(1, LANES)
    h, xn2, rmeta, gmeta, cnt = _merge(
        ya.reshape(t, A_WIDTH), ym.reshape(t, M_WIDTH), gam, x2, w_branch.astype(BF16), w_out.astype(BF16),
        norm_ffn_w.reshape(1, d), wr, br)

    dest, blk_expert, n_used, tail_start, tail_n, n_blk = _route_plan(rmeta, cnt, t)
    x_rows = _push(xn2, dest, tail_start, tail_n, n_used, n_blk * MOE_BLOCK)
    y_rows = _expert(x_rows, blk_expert, n_used, w_mlp1, b_mlp1.reshape(N_EXPERTS, 1, -1),
                     w_mlp2, b_mlp2.reshape(N_EXPERTS, 1, -1))
    out = _final(h, gmeta, norm_out_w.reshape(1, d), y_rows, dest)
    return out.reshape(b, s, d)


def kernel(x, norm_mix_w, w_in, m_conv_w, m_conv_b, m_gate_bias, m_head_norm_w, w_branch, w_out,
           norm_ffn_w, w_router, b_router, w_mlp1, b_mlp1, w_mlp2, b_mlp2, norm_final_w):
    depth = norm_mix_w.shape[0]
    assert depth == 1, "the final RMSNorm is fused into the layer's last kernel"
    return _layer(x, norm_mix_w[0], w_in[0], m_conv_w[0], m_conv_b[0], m_gate_bias[0], m_head_norm_w[0],
                  w_branch[0], w_out[0], norm_ffn_w[0], w_router[0], b_router[0], w_mlp1[0], b_mlp1[0],
                  w_mlp2[0], b_mlp2[0], norm_final_w)
```
